```python
import math
import jax, jax.numpy as jnp
from jax import lax
import numpy as np

D_MODEL = 1024
BATCH = 8
SEQ = 8192
DEPTH = 2

CHUNK = 64
HEAD_DIM = 64
D_A = 384
HEADS_A = D_A // HEAD_DIM
D_B = 256
HEADS_B = D_B // HEAD_DIM
D_C = 384
HEADS_C = D_C // HEAD_DIM
D_MIX = D_A + D_B + D_C
IN_COLS = 2 * D_A + 3 * D_B + 2 * D_C
CONV_A_WIDTH = 31
CONV_B_WIDTH = 3
SG_BLOCK = 128
PEER_HEADS = 8
D_KEY = 256
N_KEYS = 128
N_EXPERTS = N_KEYS * N_KEYS
TOPK = 16
PEER_BLOCK = 128
PLE_DIM = 256
EPS = 1e-6

kernel_name = "hybrid_conv_sgmlp_peer_trunk"


def rmsnorm(x, g):
    xf = x.astype(jnp.float32)
    y = xf * lax.rsqrt(jnp.mean(xf * xf, axis=-1, keepdims=True) + EPS)
    return (y * g.astype(jnp.float32)).astype(x.dtype)


def group_layernorm(x, g, b, heads):
    shp = x.shape
    xf = x.astype(jnp.float32).reshape(shp[:-1] + (heads, shp[-1] // heads))
    m = jnp.mean(xf, axis=-1, keepdims=True)
    var = jnp.mean(jnp.square(xf - m), axis=-1, keepdims=True)
    y = ((xf - m) * lax.rsqrt(var + EPS)).reshape(shp)
    return (y * g.astype(jnp.float32) + b.astype(jnp.float32)).astype(x.dtype)


def causal_dwconv(x, w):
    k = w.shape[0]
    return lax.conv_general_dilated(
        x, w.astype(x.dtype)[:, None, :], window_strides=(1,), padding=[(k - 1, 0)],
        dimension_numbers=("NWC", "WIO", "NWC"), feature_group_count=x.shape[-1])


def conformer_conv(za, conv_w, conv_b, ln_g, ln_b):
    val, gate = jnp.split(za, 2, axis=-1)
    y = val * jax.nn.sigmoid(gate)
    y = causal_dwconv(y, conv_w) + conv_b.astype(y.dtype)
    y = group_layernorm(y, ln_g, ln_b, HEADS_A)
    return jax.nn.silu(y)


def short_gated_conv(zb, conv_w):
    bg, cg, xb = jnp.split(zb, 3, axis=-1)
    return bg * causal_dwconv(cg * xb, conv_w)


def spatial_gating(zc, ln_g, ln_b, w_s, b_s):
    u, v = jnp.split(zc, 2, axis=-1)
    bsz, s, _ = v.shape
    v = group_layernorm(v, ln_g, ln_b, HEADS_C)
    vb = v.reshape(bsz, s // SG_BLOCK, SG_BLOCK, HEADS_C, HEAD_DIM)
    pos = jnp.arange(SG_BLOCK)
    mask = (pos[None, :] // CHUNK) <= (pos[:, None] // CHUNK)
    ws = jnp.where(mask[None], w_s, 0).astype(v.dtype)
    mixed = jnp.einsum("hij,bnjhc->bnihc", ws, vb) + b_s.T.astype(v.dtype)[None, None, :, :, None]
    return u * mixed.reshape(bsz, s, D_C)


def peer(n, w_q, sub_keys, expert_u, expert_v):
    bsz, s, d = n.shape
    t = bsz * s
    nf = n.reshape(t, d)
    q = (nf @ w_q).reshape(t, PEER_HEADS, D_KEY)
    q1, q2 = q[..., : D_KEY // 2], q[..., D_KEY // 2:]
    s1 = jnp.einsum("thd,hkd->thk", q1, sub_keys[:, 0])
    s2 = jnp.einsum("thd,hkd->thk", q2, sub_keys[:, 1])
    v1, i1 = lax.top_k(s1, TOPK)
    v2, i2 = lax.top_k(s2, TOPK)
    cand = (v1[..., :, None] + v2[..., None, :]).reshape(t, PEER_HEADS, TOPK * TOPK)
    vals, ci = lax.top_k(cand, TOPK)
    e1 = jnp.take_along_axis(i1, ci // TOPK, axis=-1)
    e2 = jnp.take_along_axis(i2, ci % TOPK, axis=-1)
    idx = (e1 * N_KEYS + e2).reshape(t, PEER_HEADS * TOPK)
    gates = jax.nn.softmax(vals.astype(jnp.float32), axis=-1).astype(n.dtype)
    gates = gates.reshape(t, PEER_HEADS * TOPK)
    nblk = t // PEER_BLOCK

    def block(args):
        xb, ib, gb = args
        ug = expert_u[ib]
        act = jax.nn.gelu(jnp.einsum("td,tkd->tk", xb, ug)) * gb
        vg = expert_v[ib]
        return jnp.einsum("tk,tkd->td", act, vg)

    out = lax.map(block, (nf.reshape(nblk, PEER_BLOCK, d),
                          idx.reshape(nblk, PEER_BLOCK, PEER_HEADS * TOPK),
                          gates.reshape(nblk, PEER_BLOCK, PEER_HEADS * TOPK)))
    return out.reshape(bsz, s, d)


def setup_inputs(seed: int = 0) -> dict:
    key = jax.random.key(seed)
    ks = jax.random.split(key, 24)
    L, D = DEPTH, D_MODEL
    nrm = lambda k, shp, sc: jax.random.normal(k, shp, jnp.float32) * sc
    gain = lambda k, shp: 1.0 + 0.05 * jax.random.normal(k, shp, jnp.float32)
    return {
        "x": nrm(ks[0], (BATCH, SEQ, D), 1.0),
        "p": nrm(ks[1], (DEPTH, BATCH, SEQ, PLE_DIM), 1.0),
        "g_mix": gain(ks[2], (L, D)),
        "w_in": nrm(ks[3], (L, D, IN_COLS), D ** -0.5),
        "conv_a_w": nrm(ks[4], (L, CONV_A_WIDTH, D_A), CONV_A_WIDTH ** -0.5),
        "conv_a_b": nrm(ks[5], (L, D_A), 0.02),
        "ln_a_g": gain(ks[6], (L, D_A)),
        "ln_a_b": nrm(ks[7], (L, D_A), 0.02),
        "conv_b_w": nrm(ks[8], (L, CONV_B_WIDTH, D_B), CONV_B_WIDTH ** -0.5),
        "ln_c_g": gain(ks[9], (L, D_C)),
        "ln_c_b": nrm(ks[10], (L, D_C), 0.02),
        "w_s": nrm(ks[11], (L, HEADS_C, SG_BLOCK, SG_BLOCK), 0.5 * SG_BLOCK ** -0.5),
        "b_s": gain(ks[12], (L, HEADS_C, SG_BLOCK)),
        "w_out": nrm(ks[13], (L, D_MIX, D), D_MIX ** -0.5),
        "g_ffn": gain(ks[14], (L, D)),
        "w_q": nrm(ks[15], (L, D, PEER_HEADS * D_KEY), D ** -0.5),
        "sub_keys": nrm(ks[16], (L, PEER_HEADS, 2, N_KEYS, D_KEY // 2), (D_KEY // 2) ** -0.5),
        "expert_u": nrm(ks[17], (L, N_EXPERTS, D), D ** -0.5),
        "expert_v": nrm(ks[18], (L, N_EXPERTS, D), 0.1),
        "g_ple": gain(ks[19], (L, D)),
        "w_pe": nrm(ks[20], (L, PLE_DIM, D), PLE_DIM ** -0.5),
        "w_pg": nrm(ks[21], (L, D, D), D ** -0.5),
        "g_final": gain(ks[22], (D,)),
    }


def reference(x, p, g_mix, w_in, conv_a_w, conv_a_b, ln_a_g, ln_a_b, conv_b_w, ln_c_g, ln_c_b,
              w_s, b_s, w_out, g_ffn, w_q, sub_keys, expert_u, expert_v, g_ple, w_pe, w_pg,
              g_final):
    h = x
    for i in range(DEPTH):
        a = rmsnorm(h, g_mix[i])
        z = a @ w_in[i]
        za = z[..., : 2 * D_A]
        zb = z[..., 2 * D_A: 2 * D_A + 3 * D_B]
        zc = z[..., 2 * D_A + 3 * D_B:]
        ya = conformer_conv(za, conv_a_w[i], conv_a_b[i], ln_a_g[i], ln_a_b[i])
        yb = short_gated_conv(zb, conv_b_w[i])
        yc = spatial_gating(zc, ln_c_g[i], ln_c_b[i], w_s[i], b_s[i])
        h = h + jnp.concatenate([ya, yb, yc], axis=-1) @ w_out[i]
        h = h + peer(rmsnorm(h, g_ffn[i]), w_q[i], sub_keys[i], expert_u[i], expert_v[i])
        gate = jax.nn.sigmoid(rmsnorm(h, g_ple[i]) @ w_pg[i])
        h = h + (p[i] @ w_pe[i]) * gate
    return rmsnorm(h, g_final)
```

```python
import functools

import jax
import jax.numpy as jnp
from jax import lax
from jax.experimental import pallas as pl
from jax.experimental.pallas import tpu as pltpu

EPS = 1e-6
HEAD_DIM = 64
CHUNK = 64
SG_BLOCK = 128
TOPK = 16
LANES = 128
SUBLANES = 8
HALO_A = 32
HALO_B = 8
VMEM_LIMIT = 56 * 1024 * 1024

F32 = jnp.float32
BF16 = jnp.bfloat16


def _dot(a, b):
    return jnp.dot(a, b, preferred_element_type=F32)


def _dot_nt(a, b):
    return lax.dot_general(a, b, (((1,), (1,)), ((), ())), preferred_element_type=F32)


def _rmsnorm(x, g):
    return x * lax.rsqrt(jnp.mean(x * x, axis=-1, keepdims=True) + EPS) * g


def _dot_hilo(x, w):
    hi = x.astype(BF16)
    lo = (x - hi.astype(F32)).astype(BF16)
    return _dot(hi, w) + _dot(lo, w)


def _group_layernorm(x, gavg, g, b):
    mean = _dot_hilo(x, gavg)
    d = x - mean
    var = _dot_hilo(d * d, gavg)
    return d * lax.rsqrt(var + EPS) * g + b


def _gelu_tanh(x):
    c = 0.7978845608028654
    return 0.5 * x * (1.0 + jnp.tanh(c * (x + 0.044715 * (x * x * x))))


def _mixer_kernel(h_ref, gmix_ref, win_ref, caw_ref, cab_ref, lag_ref, lab_ref, cbw_ref,
                  lcg_ref, lcb_ref, wss_ref, bsx_ref, gavg_ref, wout_ref, o_ref,
                  ybuf, cbuf, ya_buf, *, d_a, d_b, d_c, ka, kb, ts, rows):
    @pl.when(pl.program_id(1) == 0)
    def _():
        ybuf[0:HALO_A, :] = jnp.zeros((HALO_A, d_a), F32)
        cbuf[0:HALO_B, :] = jnp.zeros((HALO_B, d_b), F32)

    h = h_ref[0]
    a = _rmsnorm(h, gmix_ref[...]).astype(BF16)
    z = _dot(a, win_ref[...])
    o_b = 2 * d_a
    o_c = o_b + 3 * d_b

    ybuf[HALO_A:HALO_A + ts, :] = z[:, :d_a] * jax.nn.sigmoid(z[:, d_a:2 * d_a])
    for r0 in range(0, ts, rows):
        acc = jnp.zeros((rows, d_a), F32) + cab_ref[...]
        for k in range(ka):
            off = HALO_A - (ka - 1) + k + r0
            acc = acc + ybuf[off:off + rows, :] * caw_ref[k:k + 1, :]
        ya_buf[r0:r0 + rows, :] = acc
    ybuf[0:HALO_A, :] = ybuf[ts:ts + HALO_A, :]
    ya = _group_layernorm(ya_buf[...], gavg_ref[...], lag_ref[...], lab_ref[...])
    ya = ya * jax.nn.sigmoid(ya)

    cbuf[HALO_B:HALO_B + ts, :] = z[:, o_b + d_b:o_b + 2 * d_b] * z[:, o_b + 2 * d_b:o_c]
    cb = jnp.zeros((ts, d_b), F32)
    for k in range(kb):
        off = HALO_B - (kb - 1) + k
        cb = cb + cbuf[off:off + ts, :] * cbw_ref[k:k + 1, :]
    cbuf[0:HALO_B, :] = cbuf[ts:ts + HALO_B, :]
    yb = z[:, o_b:o_b + d_b] * cb

    u = z[:, o_c:o_c + d_c]
    vln = _group_layernorm(z[:, o_c + d_c:], gavg_ref[...], lcg_ref[...], lcb_ref[...]).astype(BF16)
    heads_c = d_c // HEAD_DIM
    lane_head = lax.broadcasted_iota(jnp.int32, (SG_BLOCK, d_c), 1) // HEAD_DIM
    yc_blocks = []
    for n in range(ts // SG_BLOCK):
        sl = slice(n * SG_BLOCK, (n + 1) * SG_BLOCK)
        r = _dot(wss_ref[...], vln[sl, :])
        mixed = bsx_ref[...]
        for hd in range(heads_c):
            mixed = mixed + jnp.where(lane_head == hd, r[hd * SG_BLOCK:(hd + 1) * SG_BLOCK, :], 0.0)
        yc_blocks.append(u[sl, :] * mixed)
    yc = jnp.concatenate(yc_blocks, axis=0)

    ycat = jnp.concatenate([ya, yb, yc], axis=1).astype(BF16)
    o_ref[0] = h + _dot(ycat, wout_ref[...])


def _mixer(h, g_mix, w_in, conv_a_w, conv_a_b, ln_a_g, ln_a_b, conv_b_w, ln_c_g, ln_c_b,
           wss, bsx, gavg, w_out, *, ts):
    bsz, seq, d = h.shape
    ka, d_a = conv_a_w.shape
    kb, d_b = conv_b_w.shape
    d_c = ln_c_g.shape[-1]
    assert d_a == d_c and seq % ts == 0 and ts % SG_BLOCK == 0
    row = lambda v: v.reshape(1, -1)
    full = lambda arr: pl.BlockSpec(arr.shape, lambda b, s: (0,) * arr.ndim)
    operands = (row(g_mix), w_in, conv_a_w, row(conv_a_b), row(ln_a_g), row(ln_a_b), conv_b_w,
                row(ln_c_g), row(ln_c_b), wss, bsx, gavg, w_out)
    blk = pl.BlockSpec((1, ts, d), lambda b, s: (b, s, 0))
    return pl.pallas_call(
        functools.partial(_mixer_kernel, d_a=d_a, d_b=d_b, d_c=d_c, ka=ka, kb=kb, ts=ts, rows=64),
        grid=(bsz, seq // ts),
        in_specs=[blk] + [full(o) for o in operands],
        out_specs=blk,
        out_shape=jax.ShapeDtypeStruct(h.shape, F32),
        scratch_shapes=[pltpu.VMEM((ts + HALO_A, d_a), F32),
                        pltpu.VMEM((ts + HALO_B, d_b), F32),
                        pltpu.VMEM((ts, d_a), F32)],
        compiler_params=pltpu.CompilerParams(
            dimension_semantics=("arbitrary", "arbitrary"), vmem_limit_bytes=VMEM_LIMIT),
        name="mixer",
    )(h, *operands)


def _top_values(s, k, want_rank):
    vals = []
    rank = jnp.full(s.shape, float(k), F32) if want_rank else None
    for i in range(k):
        m = jnp.max(s, axis=0, keepdims=True)
        eq = s == m
        if want_rank:
            rank = jnp.where(eq, float(i), rank)
        s = jnp.where(eq, -jnp.inf, s)
        vals.append(m)
    return jnp.concatenate(vals, axis=0), rank


def _route_kernel(h_ref, g_ref, wqt_ref, k1_ref, k2_ref,
                  xn_ref, nrow_ref, p1_ref, rho_ref, p2_ref, q_scr, *, heads, dk, tm):
    xn = _rmsnorm(h_ref[...], g_ref[...]).astype(BF16)
    xn_ref[...] = xn
    q_scr[...] = _dot_nt(wqt_ref[...], xn)

    def head_body(hd, carry):
        base = pl.multiple_of(hd * (2 * dk), 2 * dk)
        q1 = q_scr[pl.ds(base, dk), :].astype(BF16)
        q2 = q_scr[pl.ds(base + dk, dk), :].astype(BF16)
        s1 = _dot(k1_ref[hd], q1)
        s2 = _dot(k2_ref[hd], q2)
        for c in range(tm // LANES):
            cs = slice(c * LANES, (c + 1) * LANES)
            a1 = s1[:, cs]
            a2 = s2[:, cs]
            v1, _ = _top_values(a1, TOPK, False)
            v2, rho2 = _top_values(a2, TOPK, True)
            cand = jnp.concatenate([v1[a:a + 1, :] + v2 for a in range(TOPK)], axis=0)
            tv, _ = _top_values(cand, TOPK, False)
            tau = tv[TOPK - 1:TOPK, :]
            cmax = tv[0:1, :]
            z = jnp.sum(jnp.where(cand >= tau, jnp.exp(cand - cmax), 0.0), axis=0, keepdims=True)
            nrow = jnp.zeros(a1.shape, F32)
            for b in range(TOPK):
                nrow = nrow + jnp.where(a1 + v2[b:b + 1, :] >= tau, 1.0, 0.0)
            nrow_ref[hd, :, cs] = nrow
            p1_ref[hd, :, cs] = jnp.exp(a1 - v1[0:1, :]) / z
            rho_ref[hd, :, cs] = rho2
            p2_ref[hd, :, cs] = jnp.exp(a2 - v2[0:1, :])
        return carry

    lax.fori_loop(0, heads, head_body, 0)


def _route(h2d, g_ffn, wqt, k1, k2, *, tm):
    t, d = h2d.shape
    heads, nk, dk = k1.shape
    assert t % tm == 0 and tm % LANES == 0
    side = jax.ShapeDtypeStruct((heads, nk, t), F32)
    side_spec = pl.BlockSpec((heads, nk, tm), lambda i: (0, 0, i))
    return pl.pallas_call(
        functools.partial(_route_kernel, heads=heads, dk=dk, tm=tm),
        grid=(t // tm,),
        in_specs=[pl.BlockSpec((tm, d), lambda i: (i, 0)),
                  pl.BlockSpec((1, d), lambda i: (0, 0)),
                  pl.BlockSpec(wqt.shape, lambda i: (0, 0)),
                  pl.BlockSpec(k1.shape, lambda i: (0, 0, 0)),
                  pl.BlockSpec(k2.shape, lambda i: (0, 0, 0))],
        out_specs=[pl.BlockSpec((tm, d), lambda i: (i, 0)), side_spec, side_spec, side_spec, side_spec],
        out_shape=[jax.ShapeDtypeStruct((t, d), BF16), side, side, side, side],
        scratch_shapes=[pltpu.VMEM((wqt.shape[0], tm), F32)],
        compiler_params=pltpu.CompilerParams(
            dimension_semantics=("arbitrary",), vmem_limit_bytes=VMEM_LIMIT),
        name="route",
    )(h2d, g_ffn.reshape(1, d), wqt, k1, k2)


def _peer_kernel(xn_ref, h_ref, u_ref, vt_ref, nrow_ref, p1_ref, rho_ref, p2_ref, o_ref,
                 acc, s_scr, act_scr, *, heads, nk, tm, te):
    j = pl.program_id(1)

    @pl.when(j == 0)
    def _():
        acc[...] = jnp.zeros(acc.shape, F32)

    s_scr[...] = _dot_nt(u_ref[...], xn_ref[...])
    groups_per_tile = te // (nk * SUBLANES)

    def group_body(g, carry):
        r0 = pl.multiple_of((j * groups_per_tile + g) * SUBLANES, SUBLANES)
        for c in range(tm // LANES):
            cs = slice(c * LANES, (c + 1) * LANES)
            nr8 = [nrow_ref[hd, pl.ds(r0, SUBLANES), cs] for hd in range(heads)]
            pr8 = [p1_ref[hd, pl.ds(r0, SUBLANES), cs] for hd in range(heads)]
            for k in range(SUBLANES):
                lo = pl.multiple_of((g * SUBLANES + k) * nk, nk)
                w = jnp.zeros((nk, LANES), F32)
                for hd in range(heads):
                    keep = rho_ref[hd, :, cs] < nr8[hd][k:k + 1, :]
                    w = w + jnp.where(keep, p2_ref[hd, :, cs], 0.0) * pr8[hd][k:k + 1, :]
                act = _gelu_tanh(s_scr[pl.ds(lo, nk), cs]) * w
                act_scr[pl.ds(lo, nk), cs] = act.astype(BF16)
        return carry

    lax.fori_loop(0, groups_per_tile, group_body, 0)
    acc[...] += _dot(vt_ref[...], act_scr[...])

    @pl.when(j == pl.num_programs(1) - 1)
    def _():
        o_ref[...] = h_ref[...] + acc[...].T


def _peer(xn, h2d, u_bf, vt_bf, nrow, p1, rho, p2, *, tm, te):
    t, d = h2d.shape
    n_exp = u_bf.shape[0]
    heads, nk, _ = nrow.shape
    assert t % tm == 0 and n_exp % te == 0 and te % nk == 0 and n_exp == nk * nk
    side_spec = pl.BlockSpec((heads, nk, tm), lambda i, j: (0, 0, i))
    return pl.pallas_call(
        functools.partial(_peer_kernel, heads=heads, nk=nk, tm=tm, te=te),
        grid=(t // tm, n_exp // te),
        in_specs=[pl.BlockSpec((tm, d), lambda i, j: (i, 0)),
                  pl.BlockSpec((tm, d), lambda i, j: (i, 0)),
                  pl.BlockSpec((te, d), lambda i, j: (j, 0)),
                  pl.BlockSpec((d, te), lambda i, j: (0, j)),
                  side_spec, side_spec, side_spec, side_spec],
        out_specs=pl.BlockSpec((tm, d), lambda i, j: (i, 0)),
        out_shape=jax.ShapeDtypeStruct((t, d), F32),
        scratch_shapes=[pltpu.VMEM((d, tm), F32),
                        pltpu.VMEM((te, tm), F32),
                        pltpu.VMEM((te, tm), BF16)],
        compiler_params=pltpu.CompilerParams(
            dimension_semantics=("arbitrary", "arbitrary"), vmem_limit_bytes=VMEM_LIMIT),
        name="peer",
    )(xn, h2d, u_bf, vt_bf, nrow, p1, rho, p2)


def _ple_kernel(h_ref, p_ref, g_ref, wpg_ref, wpe_ref, gfin_ref, o_ref, *, final):
    h = h_ref[...]
    gate = jax.nn.sigmoid(_dot(_rmsnorm(h, g_ref[...]).astype(BF16), wpg_ref[...]))
    out = h + _dot(p_ref[...].astype(BF16), wpe_ref[...]) * gate
    if final:
        out = _rmsnorm(out, gfin_ref[...])
    o_ref[...] = out


def _ple(h2d, p2d, g_ple, w_pg, w_pe, g_final, *, tm, final):
    t, d = h2d.shape
    dp = p2d.shape[1]
    return pl.pallas_call(
        functools.partial(_ple_kernel, final=final),
        grid=(t // tm,),
        in_specs=[pl.BlockSpec((tm, d), lambda i: (i, 0)),
                  pl.BlockSpec((tm, dp), lambda i: (i, 0)),
                  pl.BlockSpec((1, d), lambda i: (0, 0)),
                  pl.BlockSpec((d, d), lambda i: (0, 0)),
                  pl.BlockSpec((dp, d), lambda i: (0, 0)),
                  pl.BlockSpec((1, d), lambda i: (0, 0))],
        out_specs=pl.BlockSpec((tm, d), lambda i: (i, 0)),
        out_shape=jax.ShapeDtypeStruct((t, d), F32),
        compiler_params=pltpu.CompilerParams(
            dimension_semantics=("arbitrary",), vmem_limit_bytes=VMEM_LIMIT),
        name="ple_final" if final else "ple",
    )(h2d, p2d, g_ple.reshape(1, d), w_pg, w_pe, g_final.reshape(1, d))


def kernel(x, p, g_mix, w_in, conv_a_w, conv_a_b, ln_a_g, ln_a_b, conv_b_w, ln_c_g, ln_c_b, w_s, b_s, w_out, g_ffn, w_q, sub_keys, expert_u, expert_v, g_ple, w_pe, w_pg, g_final):
    depth = w_in.shape[0]
    bsz, seq, d = x.shape
    t = bsz * seq
    d_c = ln_c_g.shape[-1]
    heads_c = d_c // HEAD_DIM

    lane = jnp.arange(d_c)
    gavg = jnp.where(lane[:, None] // HEAD_DIM == lane[None, :] // HEAD_DIM, 1.0 / HEAD_DIM, 0.0).astype(BF16)
    pos = jnp.arange(SG_BLOCK)
    causal = (pos[None, :] // CHUNK) <= (pos[:, None] // CHUNK)

    h = x
    for i in range(depth):
        wss = jnp.where(causal[None], w_s[i], 0.0).astype(BF16).reshape(heads_c * SG_BLOCK, SG_BLOCK)
        bsx = jnp.repeat(b_s[i].T, HEAD_DIM, axis=1)
        h = _mixer(h, g_mix[i], w_in[i].astype(BF16), conv_a_w[i], conv_a_b[i], ln_a_g[i], ln_a_b[i],
                   conv_b_w[i], ln_c_g[i], ln_c_b[i], wss, bsx, gavg, w_out[i].astype(BF16), ts=512)
        h2d = h.reshape(t, d)
        xn, nrow, p1, rho, p2 = _route(h2d, g_ffn[i], w_q[i].T.astype(BF16),
                                       sub_keys[i, :, 0].astype(BF16), sub_keys[i, :, 1].astype(BF16),
                                       tm=256)
        h2d = _peer(xn, h2d, expert_u[i].astype(BF16), expert_v[i].T.astype(BF16),
                    nrow, p1, rho, p2, tm=512, te=2048)
        h2d = _ple(h2d, p[i].reshape(t, -1), g_ple[i], w_pg[i].astype(BF16), w_pe[i].astype(BF16),
                   g_final, tm=512, final=(i == depth - 1))
        h = h2d.reshape(bsz, seq, d)
    return h
```

```python
import functools

import jax
import jax.numpy as jnp
from jax import lax
from jax.experimental import pallas as pl
from jax.experimental.pallas import tpu as pltpu

EPS = 1e-6
HEAD_DIM = 64
CHUNK = 64
SG_BLOCK = 128
TOPK = 16
LANES = 128
SUBLANES = 8
HALO_A = 32
HALO_B = 8
VMEM_LIMIT = 56 * 1024 * 1024

F32 = jnp.float32
BF16 = jnp.bfloat16


def _dot(a, b):
    return jnp.dot(a, b, preferred_element_type=F32)


def _dot_nt(a, b):
    return lax.dot_general(a, b, (((1,), (1,)), ((), ())), preferred_element_type=F32)


def _pack_rows(w):
    b = lax.bitcast_convert_type(w.astype(BF16), jnp.uint16).astype(jnp.uint32)
    return b[0::2] | (b[1::2] << 16)


def _rmsnorm(x, g):
    return x * lax.rsqrt(jnp.mean(x * x, axis=-1, keepdims=True) + EPS) * g


def _dot_hilo(x, w):
    hi = x.astype(BF16)
    lo = (x - hi.astype(F32)).astype(BF16)
    return _dot(hi, w) + _dot(lo, w)


def _group_layernorm(x, gavg, g, b):
    mean = _dot_hilo(x, gavg)
    d = x - mean
    var = _dot_hilo(d * d, gavg)
    return d * lax.rsqrt(var + EPS) * g + b


def _gelu_tanh(x):
    c = 0.7978845608028654
    h = 0.5 * x
    return h + h * jnp.tanh(x * (c + (c * 0.044715) * (x * x)))


def _mixer_kernel(h_ref, gmix_ref, win_ref, caw_ref, cab_ref, lag_ref, lab_ref, cbw_ref,
                  lcg_ref, lcb_ref, wss_ref, bsx_ref, gavg_ref, wout_ref, o_ref,
                  ybuf, cbuf, ya_buf, *, d_a, d_b, d_c, ka, kb, ts, rows):
    @pl.when(pl.program_id(1) == 0)
    def _():
        ybuf[0:HALO_A, :] = jnp.zeros((HALO_A, d_a), F32)
        cbuf[0:HALO_B, :] = jnp.zeros((HALO_B, d_b), F32)

    h = h_ref[0]
    a = _rmsnorm(h, gmix_ref[...]).astype(BF16)
    z = _dot(a, win_ref[...])
    o_b = 2 * d_a
    o_c = o_b + 3 * d_b

    ybuf[HALO_A:HALO_A + ts, :] = z[:, :d_a] * jax.nn.sigmoid(z[:, d_a:2 * d_a])
    for r0 in range(0, ts, rows):
        acc = jnp.zeros((rows, d_a), F32) + cab_ref[...]
        for k in range(ka):
            off = HALO_A - (ka - 1) + k + r0
            acc = acc + ybuf[off:off + rows, :] * caw_ref[k:k + 1, :]
        ya_buf[r0:r0 + rows, :] = acc
    ybuf[0:HALO_A, :] = ybuf[ts:ts + HALO_A, :]
    ya = _group_layernorm(ya_buf[...], gavg_ref[...], lag_ref[...], lab_ref[...])
    ya = ya * jax.nn.sigmoid(ya)

    cbuf[HALO_B:HALO_B + ts, :] = z[:, o_b + d_b:o_b + 2 * d_b] * z[:, o_b + 2 * d_b:o_c]
    cb = jnp.zeros((ts, d_b), F32)
    for k in range(kb):
        off = HALO_B - (kb - 1) + k
        cb = cb + cbuf[off:off + ts, :] * cbw_ref[k:k + 1, :]
    cbuf[0:HALO_B, :] = cbuf[ts:ts + HALO_B, :]
    yb = z[:, o_b:o_b + d_b] * cb

    u = z[:, o_c:o_c + d_c]
    vln = _group_layernorm(z[:, o_c + d_c:], gavg_ref[...], lcg_ref[...], lcb_ref[...]).astype(BF16)
    heads_c = d_c // HEAD_DIM
    lane_head = lax.broadcasted_iota(jnp.int32, (SG_BLOCK, d_c), 1) // HEAD_DIM
    yc_blocks = []
    for n in range(ts // SG_BLOCK):
        sl = slice(n * SG_BLOCK, (n + 1) * SG_BLOCK)
        r = _dot(wss_ref[...], vln[sl, :])
        mixed = bsx_ref[...]
        for hd in range(heads_c):
            mixed = mixed + jnp.where(lane_head == hd, r[hd * SG_BLOCK:(hd + 1) * SG_BLOCK, :], 0.0)
        yc_blocks.append(u[sl, :] * mixed)
    yc = jnp.concatenate(yc_blocks, axis=0)

    ycat = jnp.concatenate([ya, yb, yc], axis=1).astype(BF16)
    o_ref[0] = h + _dot(ycat, wout_ref[...])


def _mixer(h, g_mix, w_in, conv_a_w, conv_a_b, ln_a_g, ln_a_b, conv_b_w, ln_c_g, ln_c_b,
           wss, bsx, gavg, w_out, *, ts):
    bsz, seq, d = h.shape
    ka, d_a = conv_a_w.shape
    kb, d_b = conv_b_w.shape
    d_c = ln_c_g.shape[-1]
    assert d_a == d_c and seq % ts == 0 and ts % SG_BLOCK == 0
    row = lambda v: v.reshape(1, -1)
    full = lambda arr: pl.BlockSpec(arr.shape, lambda b, s: (0,) * arr.ndim)
    operands = (row(g_mix), w_in, conv_a_w, row(conv_a_b), row(ln_a_g), row(ln_a_b), conv_b_w,
                row(ln_c_g), row(ln_c_b), wss, bsx, gavg, w_out)
    blk = pl.BlockSpec((1, ts, d), lambda b, s: (b, s, 0))
    return pl.pallas_call(
        functools.partial(_mixer_kernel, d_a=d_a, d_b=d_b, d_c=d_c, ka=ka, kb=kb, ts=ts, rows=64),
        grid=(bsz, seq // ts),
        in_specs=[blk] + [full(o) for o in operands],
        out_specs=blk,
        out_shape=jax.ShapeDtypeStruct(h.shape, F32),
        scratch_shapes=[pltpu.VMEM((ts + HALO_A, d_a), F32),
                        pltpu.VMEM((ts + HALO_B, d_b), F32),
                        pltpu.VMEM((ts, d_a), F32)],
        compiler_params=pltpu.CompilerParams(
            dimension_semantics=("arbitrary", "arbitrary"), vmem_limit_bytes=VMEM_LIMIT),
        name="mixer",
    )(h, *operands)


def _top_values(s, k, want_rank):
    vals = []
    rank = jnp.full(s.shape, float(k), F32) if want_rank else None
    for i in range(k):
        m = jnp.max(s, axis=0, keepdims=True)
        eq = s == m
        if want_rank:
            rank = jnp.where(eq, float(i), rank)
        s = jnp.where(eq, -jnp.inf, s)
        vals.append(m)
    return jnp.concatenate(vals, axis=0), rank


def _route_kernel(h_ref, g_ref, wqt_ref, k1_ref, k2_ref,
                  xn_ref, nrow_ref, p1_ref, rho_ref, p2_ref, q_scr, *, heads, dk, tm):
    xnt = _rmsnorm(h_ref[...], g_ref[...]).T.astype(BF16)
    xn_ref[...] = pltpu.bitcast(xnt, jnp.uint32)
    q_scr[...] = _dot(wqt_ref[...], xnt)

    def head_body(hd, carry):
        base = pl.multiple_of(hd * (2 * dk), 2 * dk)
        q1 = q_scr[pl.ds(base, dk), :].astype(BF16)
        q2 = q_scr[pl.ds(base + dk, dk), :].astype(BF16)
        s1 = _dot(k1_ref[hd], q1)
        s2 = _dot(k2_ref[hd], q2)
        for c in range(tm // LANES):
            cs = slice(c * LANES, (c + 1) * LANES)
            a1 = s1[:, cs]
            a2 = s2[:, cs]
            v1, _ = _top_values(a1, TOPK, False)
            v2, rho2 = _top_values(a2, TOPK, True)
            hs = TOPK // 2
            cand = jnp.concatenate(
                [v1[0:1, :] + v2]
                + [v1[a:a + 1, :] + v2[0:hs, :] for a in range(1, hs)]
                + [v1[hs:, :] + v2[0:1, :]], axis=0)
            tv, _ = _top_values(cand, TOPK, False)
            tau = tv[TOPK - 1:TOPK, :]
            cmax = tv[0:1, :]
            z = jnp.sum(jnp.where(cand >= tau, jnp.exp(cand - cmax), 0.0), axis=0, keepdims=True)
            nrow = jnp.zeros(a1.shape, F32)
            for b in range(TOPK):
                nrow = nrow + jnp.where(a1 + v2[b:b + 1, :] >= tau, 1.0, 0.0)
            nrow_ref[hd, :, cs] = nrow
            p1_ref[hd, :, cs] = jnp.exp(a1 - v1[0:1, :]) / z
            rho_ref[hd, :, cs] = pltpu.bitcast(rho2.astype(BF16), jnp.uint32)
            p2_ref[hd, :, cs] = pltpu.bitcast(jnp.exp(a2 - v2[0:1, :]).astype(BF16), jnp.uint32)
        return carry

    lax.fori_loop(0, heads, head_body, 0)


def _route(h2d, g_ffn, wqt, k1, k2, *, tm):
    t, d = h2d.shape
    heads, nk, dk = k1.shape
    assert t % tm == 0 and tm % LANES == 0
    side = jax.ShapeDtypeStruct((heads, nk, t), F32)
    side16 = jax.ShapeDtypeStruct((heads, nk // 2, t), jnp.uint32)
    side_spec = pl.BlockSpec((heads, nk, tm), lambda i: (0, 0, i))
    side16_spec = pl.BlockSpec((heads, nk // 2, tm), lambda i: (0, 0, i))
    return pl.pallas_call(
        functools.partial(_route_kernel, heads=heads, dk=dk, tm=tm),
        grid=(t // tm,),
        in_specs=[pl.BlockSpec((tm, d), lambda i: (i, 0)),
                  pl.BlockSpec((1, d), lambda i: (0, 0)),
                  pl.BlockSpec(wqt.shape, lambda i: (0, 0)),
                  pl.BlockSpec(k1.shape, lambda i: (0, 0, 0)),
                  pl.BlockSpec(k2.shape, lambda i: (0, 0, 0))],
        out_specs=[pl.BlockSpec((d // 2, tm), lambda i: (0, i)), side_spec, side_spec, side16_spec, side16_spec],
        out_shape=[jax.ShapeDtypeStruct((d // 2, t), jnp.uint32), side, side, side16, side16],
        scratch_shapes=[pltpu.VMEM((wqt.shape[0], tm), F32)],
        compiler_params=pltpu.CompilerParams(
            dimension_semantics=("arbitrary",), vmem_limit_bytes=VMEM_LIMIT),
        name="route",
    )(h2d, g_ffn.reshape(1, d), wqt, k1, k2)


def _peer_kernel(xn_ref, h_ref, u_ref, vt_ref, nrow_ref, p1_ref, rho_ref, p2_ref, o_ref,
                 acc, s_a, s_b, act_a, act_b, *, heads, nk, tm, te, n_tiles):
    j = pl.program_id(1)
    half = te // 2
    rows_half = half // nk
    pack = 2 * SUBLANES

    def scores(hf, s_buf):
        u_half = pltpu.bitcast(u_ref[hf * half // 2:(hf + 1) * half // 2, :], BF16)
        s_buf[...] = _dot(u_half, pltpu.bitcast(xn_ref[...], BF16))

    def accumulate(act_buf, hf):
        vt_half = pltpu.bitcast(vt_ref[:, hf * half:(hf + 1) * half], BF16)
        acc[...] += _dot(vt_half, pltpu.bitcast(act_buf[...], BF16))

    def gate(s_buf, act_buf, tile, hf):
        r0 = pl.multiple_of(tile * SUBLANES, SUBLANES)
        for c in range(tm // LANES):
            cs = slice(c * LANES, (c + 1) * LANES)
            nr8 = [nrow_ref[hd, pl.ds(r0, SUBLANES), cs] for hd in range(heads)]
            pr8 = [p1_ref[hd, pl.ds(r0, SUBLANES), cs] for hd in range(heads)]
            for k in range(rows_half):
                kk = hf * rows_half + k
                nr = [jnp.broadcast_to(nr8[hd][kk:kk + 1, :], (pack, LANES)).astype(BF16)
                      for hd in range(heads)]
                pr = [jnp.broadcast_to(pr8[hd][kk:kk + 1, :], (pack, LANES)).astype(BF16)
                      for hd in range(heads)]
                for m in range(nk // pack):
                    ms = slice(m * SUBLANES, (m + 1) * SUBLANES)
                    w = None
                    for hd in range(heads):
                        keep = pltpu.bitcast(rho_ref[hd, ms, cs], BF16) < nr[hd]
                        p2 = pltpu.bitcast(p2_ref[hd, ms, cs], BF16)
                        term = jnp.where(keep, p2, jnp.zeros((), BF16)) * pr[hd]
                        w = term if w is None else w + term
                    rs = slice(k * nk + m * pack, k * nk + (m + 1) * pack)
                    act = _gelu_tanh(s_buf[rs, cs].astype(BF16)) * w
                    ws = slice((k * nk + m * pack) // 2, (k * nk + (m + 1) * pack) // 2)
                    act_buf[ws, cs] = pltpu.bitcast(act, jnp.uint32)

    steady = jnp.logical_and(j > 0, j < n_tiles)

    @pl.when(j == 0)
    def _():
        acc[...] = jnp.zeros(acc.shape, F32)
        scores(0, s_a)

    @pl.when(j == 0)
    def _():
        scores(1, s_b)
        gate(s_a, act_a, j, 0)

    @pl.when(steady)
    def _():
        gate(s_b, act_b, j - 1, 1)
        scores(0, s_a)
        accumulate(act_a, 0)

    @pl.when(steady)
    def _():
        gate(s_a, act_a, j, 0)
        scores(1, s_b)
        accumulate(act_b, 1)

    @pl.when(j == n_tiles)
    def _():
        gate(s_b, act_b, j - 1, 1)
        accumulate(act_a, 0)

    @pl.when(j == n_tiles)
    def _():
        accumulate(act_b, 1)
        o_ref[...] = h_ref[...] + acc[...].T


def _peer(xn, h2d, u_bf, vt_bf, nrow, p1, rho, p2, *, tm, te):
    t, d = h2d.shape
    n_exp = 2 * u_bf.shape[0]
    heads, nk, _ = nrow.shape
    assert te == nk * SUBLANES and n_exp == nk * nk and t % tm == 0 and tm % LANES == 0
    n_tiles = n_exp // te
    half = te // 2
    side_spec = pl.BlockSpec((heads, nk, tm), lambda i, j: (0, 0, i))
    side16_spec = pl.BlockSpec((heads, nk // 2, tm), lambda i, j: (0, 0, i))
    return pl.pallas_call(
        functools.partial(_peer_kernel, heads=heads, nk=nk, tm=tm, te=te, n_tiles=n_tiles),
        grid=(t // tm, n_tiles + 1),
        in_specs=[pl.BlockSpec((d // 2, tm), lambda i, j: (0, i)),
                  pl.BlockSpec((tm, d), lambda i, j: (i, 0)),
                  pl.BlockSpec((te // 2, d), lambda i, j: (jnp.minimum(j, n_tiles - 1), 0)),
                  pl.BlockSpec((d // 2, te), lambda i, j: (0, jnp.maximum(j - 1, 0))),
                  side_spec, side_spec, side16_spec, side16_spec],
        out_specs=pl.BlockSpec((tm, d), lambda i, j: (i, 0)),
        out_shape=jax.ShapeDtypeStruct((t, d), F32),
        scratch_shapes=[pltpu.VMEM((d, tm), F32),
                        pltpu.VMEM((half, tm), F32), pltpu.VMEM((half, tm), F32),
                        pltpu.VMEM((half // 2, tm), jnp.uint32),
                        pltpu.VMEM((half // 2, tm), jnp.uint32)],
        compiler_params=pltpu.CompilerParams(
            dimension_semantics=("arbitrary", "arbitrary"), vmem_limit_bytes=VMEM_LIMIT),
        name="peer",
    )(xn, h2d, u_bf, vt_bf, nrow, p1, rho, p2)


def _ple_kernel(h_ref, p_ref, g_ref, wpg_ref, wpe_ref, gfin_ref, o_ref, *, final):
    h = h_ref[...]
    gate = jax.nn.sigmoid(_dot(_rmsnorm(h, g_ref[...]).astype(BF16), wpg_ref[...]))
    out = h + _dot(p_ref[...].astype(BF16), wpe_ref[...]) * gate
    if final:
        out = _rmsnorm(out, gfin_ref[...])
    o_ref[...] = out


def _ple(h2d, p2d, g_ple, w_pg, w_pe, g_final, *, tm, final):
    t, d = h2d.shape
    dp = p2d.shape[1]
    return pl.pallas_call(
        functools.partial(_ple_kernel, final=final),
        grid=(t // tm,),
        in_specs=[pl.BlockSpec((tm, d), lambda i: (i, 0)),
                  pl.BlockSpec((tm, dp), lambda i: (i, 0)),
                  pl.BlockSpec((1, d), lambda i: (0, 0)),
                  pl.BlockSpec((d, d), lambda i: (0, 0)),
                  pl.BlockSpec((dp, d), lambda i: (0, 0)),
                  pl.BlockSpec((1, d), lambda i: (0, 0))],
        out_specs=pl.BlockSpec((tm, d), lambda i: (i, 0)),
        out_shape=jax.ShapeDtypeStruct((t, d), F32),
        compiler_params=pltpu.CompilerParams(
            dimension_semantics=("arbitrary",), vmem_limit_bytes=VMEM_LIMIT),
        name="ple_final" if final else "ple",
    )(h2d, p2d, g_ple.reshape(1, d), w_pg, w_pe, g_final.reshape(1, d))


def kernel(x, p, g_mix, w_in, conv_a_w, conv_a_b, ln_a_g, ln_a_b, conv_b_w, ln_c_g, ln_c_b, w_s, b_s, w_out, g_ffn, w_q, sub_keys, expert_u, expert_v, g_ple, w_pe, w_pg, g_final):
    depth = w_in.shape[0]
    bsz, seq, d = x.shape
    t = bsz * seq
    d_c = ln_c_g.shape[-1]
    heads_c = d_c // HEAD_DIM

    lane = jnp.arange(d_c)
    gavg = jnp.where(lane[:, None] // HEAD_DIM == lane[None, :] // HEAD_DIM, 1.0 / HEAD_DIM, 0.0).astype(BF16)
    pos = jnp.arange(SG_BLOCK)
    causal = (pos[None, :] // CHUNK) <= (pos[:, None] // CHUNK)

    h = x
    for i in range(depth):
        wss = jnp.where(causal[None], w_s[i], 0.0).astype(BF16).reshape(heads_c * SG_BLOCK, SG_BLOCK)
        bsx = jnp.repeat(b_s[i].T, HEAD_DIM, axis=1)
        h = _mixer(h, g_mix[i], w_in[i].astype(BF16), conv_a_w[i], conv_a_b[i], ln_a_g[i], ln_a_b[i],
                   conv_b_w[i], ln_c_g[i], ln_c_b[i], wss, bsx, gavg, w_out[i].astype(BF16), ts=512)
        h2d = h.reshape(t, d)
        xn, nrow, p1, rho, p2 = _route(h2d, g_ffn[i], w_q[i].T.astype(BF16),
                                       sub_keys[i, :, 0].astype(BF16), sub_keys[i, :, 1].astype(BF16),
                                       tm=256)
        h2d = _peer(xn, h2d, _pack_rows(expert_u[i]), _pack_rows(expert_v[i].T),
                    nrow, p1, rho, p2, tm=512, te=1024)
        h2d = _ple(h2d, p[i].reshape(t, -1), g_ple[i], w_pg[i].astype(BF16), w_pe[i].astype(BF16),
                   g_final, tm=512, final=(i == depth - 1))
        h = h2d.reshape(bsz, seq, d)
    return h
```

```python
import functools

import jax
import jax.numpy as jnp
from jax import lax
from jax.experimental import pallas as pl
from jax.experimental.pallas import tpu as pltpu

EPS = 1e-6
HEAD_DIM = 64
CHUNK = 64
SG_BLOCK = 128
TOPK = 16
LANES = 128
SUBLANES = 8
HALO_A = 32
HALO_B = 8
VMEM_LIMIT = 56 * 1024 * 1024

F32 = jnp.float32
BF16 = jnp.bfloat16


def _dot(a, b):
    return jnp.dot(a, b, preferred_element_type=F32)


def _dot_nt(a, b):
    return lax.dot_general(a, b, (((1,), (1,)), ((), ())), preferred_element_type=F32)


def _pack_kernel(w_ref, o_ref, *, transpose):
    w = w_ref[...]
    if transpose:
        w = w.T
    o_ref[...] = pltpu.bitcast(w.astype(BF16), jnp.uint32)


def _pack_rows(w, *, transpose=False, rows=512):
    n, d = w.shape
    rows = min(rows, n)
    assert n % rows == 0 and rows % (2 * SUBLANES) == 0
    if transpose:
        in_spec = pl.BlockSpec((rows, d), lambda i: (i, 0))
        out_spec = pl.BlockSpec((d // 2, rows), lambda i: (0, i))
        out_shape = jax.ShapeDtypeStruct((d // 2, n), jnp.uint32)
    else:
        in_spec = pl.BlockSpec((rows, d), lambda i: (i, 0))
        out_spec = pl.BlockSpec((rows // 2, d), lambda i: (i, 0))
        out_shape = jax.ShapeDtypeStruct((n // 2, d), jnp.uint32)
    return pl.pallas_call(
        functools.partial(_pack_kernel, transpose=transpose),
        grid=(n // rows,), in_specs=[in_spec], out_specs=out_spec, out_shape=out_shape,
        compiler_params=pltpu.CompilerParams(
            dimension_semantics=("arbitrary",), vmem_limit_bytes=VMEM_LIMIT),
        name="pack_t" if transpose else "pack",
    )(w)


def _rmsnorm(x, g):
    return x * lax.rsqrt(jnp.mean(x * x, axis=-1, keepdims=True) + EPS) * g


def _dot_hilo(x, w):
    hi = x.astype(BF16)
    lo = (x - hi.astype(F32)).astype(BF16)
    return _dot(hi, w) + _dot(lo, w)


def _group_layernorm(x, gavg, g, b):
    mean = _dot_hilo(x, gavg)
    d = x - mean
    var = _dot_hilo(d * d, gavg)
    return d * lax.rsqrt(var + EPS) * g + b


def _gelu_tanh(x):
    c = 0.7978845608028654
    h = 0.5 * x
    return h + h * jnp.tanh(x * (c + (c * 0.044715) * (x * x)))


def _mixer_kernel(h_ref, gmix_ref, win_ref, caw_ref, cab_ref, lag_ref, lab_ref, cbw_ref,
                  lcg_ref, lcb_ref, wss_ref, bsx_ref, gavg_ref, wout_ref, o_ref,
                  ybuf, cbuf, ya_buf, *, d_a, d_b, d_c, ka, kb, ts, rows):
    @pl.when(pl.program_id(1) == 0)
    def _():
        ybuf[0:HALO_A, :] = jnp.zeros((HALO_A, d_a), F32)
        cbuf[0:HALO_B, :] = jnp.zeros((HALO_B, d_b), F32)

    h = h_ref[0]
    a = _rmsnorm(h, gmix_ref[...]).astype(BF16)
    z = _dot(a, win_ref[...])
    o_b = 2 * d_a
    o_c = o_b + 3 * d_b

    ybuf[HALO_A:HALO_A + ts, :] = z[:, :d_a] * jax.nn.sigmoid(z[:, d_a:2 * d_a])
    for r0 in range(0, ts, rows):
        acc = jnp.zeros((rows, d_a), F32) + cab_ref[...]
        for k in range(ka):
            off = HALO_A - (ka - 1) + k + r0
            acc = acc + ybuf[off:off + rows, :] * caw_ref[k:k + 1, :]
        ya_buf[r0:r0 + rows, :] = acc
    ybuf[0:HALO_A, :] = ybuf[ts:ts + HALO_A, :]
    ya = _group_layernorm(ya_buf[...], gavg_ref[...], lag_ref[...], lab_ref[...])
    ya = ya * jax.nn.sigmoid(ya)

    cbuf[HALO_B:HALO_B + ts, :] = z[:, o_b + d_b:o_b + 2 * d_b] * z[:, o_b + 2 * d_b:o_c]
    cb = jnp.zeros((ts, d_b), F32)
    for k in range(kb):
        off = HALO_B - (kb - 1) + k
        cb = cb + cbuf[off:off + ts, :] * cbw_ref[k:k + 1, :]
    cbuf[0:HALO_B, :] = cbuf[ts:ts + HALO_B, :]
    yb = z[:, o_b:o_b + d_b] * cb

    u = z[:, o_c:o_c + d_c]
    vln = _group_layernorm(z[:, o_c + d_c:], gavg_ref[...], lcg_ref[...], lcb_ref[...]).astype(BF16)
    heads_c = d_c // HEAD_DIM
    lane_head = lax.broadcasted_iota(jnp.int32, (SG_BLOCK, d_c), 1) // HEAD_DIM
    yc_blocks = []
    for n in range(ts // SG_BLOCK):
        sl = slice(n * SG_BLOCK, (n + 1) * SG_BLOCK)
        r = _dot(wss_ref[...], vln[sl, :])
        mixed = bsx_ref[...]
        for hd in range(heads_c):
            mixed = mixed + jnp.where(lane_head == hd, r[hd * SG_BLOCK:(hd + 1) * SG_BLOCK, :], 0.0)
        yc_blocks.append(u[sl, :] * mixed)
    yc = jnp.concatenate(yc_blocks, axis=0)

    ycat = jnp.concatenate([ya, yb, yc], axis=1).astype(BF16)
    o_ref[0] = h + _dot(ycat, wout_ref[...])


def _mixer(h, g_mix, w_in, conv_a_w, conv_a_b, ln_a_g, ln_a_b, conv_b_w, ln_c_g, ln_c_b,
           wss, bsx, gavg, w_out, *, ts):
    bsz, seq, d = h.shape
    ka, d_a = conv_a_w.shape
    kb, d_b = conv_b_w.shape
    d_c = ln_c_g.shape[-1]
    assert d_a == d_c and seq % ts == 0 and ts % SG_BLOCK == 0
    row = lambda v: v.reshape(1, -1)
    full = lambda arr: pl.BlockSpec(arr.shape, lambda b, s: (0,) * arr.ndim)
    operands = (row(g_mix), w_in, conv_a_w, row(conv_a_b), row(ln_a_g), row(ln_a_b), conv_b_w,
                row(ln_c_g), row(ln_c_b), wss, bsx, gavg, w_out)
    blk = pl.BlockSpec((1, ts, d), lambda b, s: (b, s, 0))
    return pl.pallas_call(
        functools.partial(_mixer_kernel, d_a=d_a, d_b=d_b, d_c=d_c, ka=ka, kb=kb, ts=ts, rows=64),
        grid=(bsz, seq // ts),
        in_specs=[blk] + [full(o) for o in operands],
        out_specs=blk,
        out_shape=jax.ShapeDtypeStruct(h.shape, F32),
        scratch_shapes=[pltpu.VMEM((ts + HALO_A, d_a), F32),
                        pltpu.VMEM((ts + HALO_B, d_b), F32),
                        pltpu.VMEM((ts, d_a), F32)],
        compiler_params=pltpu.CompilerParams(
            dimension_semantics=("arbitrary", "arbitrary"), vmem_limit_bytes=VMEM_LIMIT),
        name="mixer",
    )(h, *operands)


def _merge_exchange_pairs(n):
    pairs = []
    t = (n - 1).bit_length()
    p = 1 << (t - 1)
    while p > 0:
        q, r, d = 1 << (t - 1), 0, p
        while True:
            pairs.extend((i, i + d) for i in range(n - d) if (i & p) == r)
            if q == p:
                break
            d, q, r = q - p, q >> 1, p
        p >>= 1
    return pairs


def _top_values(s, k):
    n = s.shape[0] // SUBLANES
    lists = [s[i * SUBLANES:(i + 1) * SUBLANES, :] for i in range(n)]
    for i, j in _merge_exchange_pairs(n):
        lists[i], lists[j] = jnp.maximum(lists[i], lists[j]), jnp.minimum(lists[i], lists[j])
    vals = []
    for r in range(k):
        m = jnp.max(lists[0], axis=0, keepdims=True)
        vals.append(m)
        if r + 1 < k:
            head = lists[0] == m
            for i in range(min(n - 1, k - 1 - r)):
                lists[i] = jnp.where(head, lists[i + 1], lists[i])
            if n - 1 < k - 1 - r:
                lists[n - 1] = jnp.where(head, -jnp.inf, lists[n - 1])
    return jnp.concatenate(vals, axis=0)


def _pair_bf16(x):
    b = pltpu.bitcast(x.astype(BF16).astype(F32), jnp.uint32)
    return b | (b >> 16)


def _route_kernel(h_ref, g_ref, wqt_ref, k1_ref, k2_ref,
                  xn_ref, nrow_ref, p1_ref, rho_ref, p2_ref, q_scr, *, heads, dk, tm):
    xnt = _rmsnorm(h_ref[...], g_ref[...]).T.astype(BF16)
    xn_ref[...] = pltpu.bitcast(xnt, jnp.uint32)
    q_scr[...] = _dot(pltpu.bitcast(wqt_ref[...], BF16), xnt)

    def head_body(hd, carry):
        base = pl.multiple_of(hd * (2 * dk), 2 * dk)
        q1 = q_scr[pl.ds(base, dk), :].astype(BF16)
        q2 = q_scr[pl.ds(base + dk, dk), :].astype(BF16)
        s1 = _dot(pltpu.bitcast(k1_ref[hd], BF16), q1)
        s2 = _dot(pltpu.bitcast(k2_ref[hd], BF16), q2)
        for c in range(tm // LANES):
            cs = slice(c * LANES, (c + 1) * LANES)
            a1 = s1[:, cs]
            a2 = s2[:, cs]
            v1 = _top_values(a1, TOPK)
            v2 = _top_values(a2, TOPK)
            rho2 = jnp.full(a2.shape, float(TOPK), F32)
            for r in reversed(range(TOPK)):
                rho2 = jnp.where(a2 >= v2[r:r + 1, :], float(r), rho2)
            hs = TOPK // 2
            cand = jnp.concatenate(
                [v1[0:1, :] + v2]
                + [v1[a:a + 1, :] + v2[0:hs, :] for a in range(1, hs)]
                + [v1[hs:, :] + v2[0:1, :]], axis=0)
            tv = _top_values(cand, TOPK)
            tau = tv[TOPK - 1:TOPK, :]
            cmax = tv[0:1, :]
            z = jnp.sum(jnp.where(cand >= tau, jnp.exp(cand - cmax), 0.0), axis=0, keepdims=True)
            top_hi = jnp.sum(jnp.where(v1[0:1, :] + v2[hs:, :] >= tau, 1.0, 0.0), axis=0, keepdims=True)
            nrow = jnp.where(a1 == v1[0:1, :], top_hi, 0.0)
            for b in range(hs):
                nrow = nrow + jnp.where(a1 + v2[b:b + 1, :] >= tau, 1.0, 0.0)
            nrow_ref[hd, :, cs] = _pair_bf16(nrow)
            p1_ref[hd, :, cs] = _pair_bf16(jnp.exp(a1 - v1[0:1, :]) / z)
            rho_ref[hd, :, cs] = pltpu.bitcast(rho2.astype(BF16), jnp.uint32)
            p2_ref[hd, :, cs] = pltpu.bitcast(jnp.exp(a2 - v2[0:1, :]).astype(BF16), jnp.uint32)
        return carry

    lax.fori_loop(0, heads, head_body, 0)


def _route(h2d, g_ffn, wqt, k1, k2, *, tm):
    t, d = h2d.shape
    heads, nk, dk = k1.shape[0], 2 * k1.shape[1], k1.shape[2]
    assert t % tm == 0 and tm % LANES == 0
    side = jax.ShapeDtypeStruct((heads, nk, t), jnp.uint32)
    side16 = jax.ShapeDtypeStruct((heads, nk // 2, t), jnp.uint32)
    side_spec = pl.BlockSpec((heads, nk, tm), lambda i: (0, 0, i))
    side16_spec = pl.BlockSpec((heads, nk // 2, tm), lambda i: (0, 0, i))
    return pl.pallas_call(
        functools.partial(_route_kernel, heads=heads, dk=dk, tm=tm),
        grid=(t // tm,),
        in_specs=[pl.BlockSpec((tm, d), lambda i: (i, 0)),
                  pl.BlockSpec((1, d), lambda i: (0, 0)),
                  pl.BlockSpec(wqt.shape, lambda i: (0, 0)),
                  pl.BlockSpec(k1.shape, lambda i: (0, 0, 0)),
                  pl.BlockSpec(k2.shape, lambda i: (0, 0, 0))],
        out_specs=[pl.BlockSpec((d // 2, tm), lambda i: (0, i)), side_spec, side_spec, side16_spec, side16_spec],
        out_shape=[jax.ShapeDtypeStruct((d // 2, t), jnp.uint32), side, side, side16, side16],
        scratch_shapes=[pltpu.VMEM((2 * wqt.shape[0], tm), F32)],
        compiler_params=pltpu.CompilerParams(
            dimension_semantics=("arbitrary",), vmem_limit_bytes=VMEM_LIMIT),
        name="route",
    )(h2d, g_ffn.reshape(1, d), wqt, k1, k2)


def _peer_kernel(xn_ref, h_ref, u_ref, vt_ref, nrow_ref, p1_ref, rho_ref, p2_ref, o_ref,
                 acc, s_a, s_b, act_a, act_b, *, heads, nk, tm, te, n_tiles):
    j = pl.program_id(1)
    half = te // 2
    rows_half = half // nk
    pack = 2 * SUBLANES

    def scores(hf, s_buf):
        u_half = pltpu.bitcast(u_ref[hf * half // 2:(hf + 1) * half // 2, :], BF16)
        s_buf[...] = _dot(u_half, pltpu.bitcast(xn_ref[...], BF16))

    def accumulate(act_buf, hf):
        vt_half = pltpu.bitcast(vt_ref[:, hf * half:(hf + 1) * half], BF16)
        acc[...] += _dot(vt_half, pltpu.bitcast(act_buf[...], BF16))

    def gate(s_buf, act_buf, tile, hf):
        r0 = pl.multiple_of(tile * SUBLANES, SUBLANES)
        for c in range(tm // LANES):
            cs = slice(c * LANES, (c + 1) * LANES)
            nr8 = [nrow_ref[hd, pl.ds(r0, SUBLANES), cs] for hd in range(heads)]
            pr8 = [p1_ref[hd, pl.ds(r0, SUBLANES), cs] for hd in range(heads)]
            for k in range(rows_half):
                kk = hf * rows_half + k
                nr = [pltpu.bitcast(jnp.broadcast_to(nr8[hd][kk:kk + 1, :], (SUBLANES, LANES)), BF16)
                      for hd in range(heads)]
                pr = [pltpu.bitcast(jnp.broadcast_to(pr8[hd][kk:kk + 1, :], (SUBLANES, LANES)), BF16)
                      for hd in range(heads)]
                for m in range(nk // pack):
                    ms = slice(m * SUBLANES, (m + 1) * SUBLANES)
                    w = None
                    for hd in range(heads):
                        keep = pltpu.bitcast(rho_ref[hd, ms, cs], BF16) < nr[hd]
                        p2 = pltpu.bitcast(p2_ref[hd, ms, cs], BF16)
                        term = jnp.where(keep, p2, jnp.zeros((), BF16)) * pr[hd]
                        w = term if w is None else w + term
                    rs = slice(k * nk + m * pack, k * nk + (m + 1) * pack)
                    act = _gelu_tanh(s_buf[rs, cs].astype(BF16)) * w
                    ws = slice((k * nk + m * pack) // 2, (k * nk + (m + 1) * pack) // 2)
                    act_buf[ws, cs] = pltpu.bitcast(act, jnp.uint32)

    steady = jnp.logical_and(j > 0, j < n_tiles)

    @pl.when(j == 0)
    def _():
        acc[...] = jnp.zeros(acc.shape, F32)
        scores(0, s_a)

    @pl.when(j == 0)
    def _():
        scores(1, s_b)
        gate(s_a, act_a, j, 0)

    @pl.when(steady)
    def _():
        gate(s_b, act_b, j - 1, 1)
        scores(0, s_a)
        accumulate(act_a, 0)

    @pl.when(steady)
    def _():
        gate(s_a, act_a, j, 0)
        scores(1, s_b)
        accumulate(act_b, 1)

    @pl.when(j == n_tiles)
    def _():
        gate(s_b, act_b, j - 1, 1)
        accumulate(act_a, 0)

    @pl.when(j == n_tiles)
    def _():
        accumulate(act_b, 1)
        o_ref[...] = h_ref[...] + acc[...].T


def _peer(xn, h2d, u_bf, vt_bf, nrow, p1, rho, p2, *, tm, te):
    t, d = h2d.shape
    n_exp = 2 * u_bf.shape[0]
    heads, nk, _ = nrow.shape
    assert te == nk * SUBLANES and n_exp == nk * nk and t % tm == 0 and tm % LANES == 0
    n_tiles = n_exp // te
    half = te // 2
    side_spec = pl.BlockSpec((heads, nk, tm), lambda i, j: (0, 0, i))
    side16_spec = pl.BlockSpec((heads, nk // 2, tm), lambda i, j: (0, 0, i))
    return pl.pallas_call(
        functools.partial(_peer_kernel, heads=heads, nk=nk, tm=tm, te=te, n_tiles=n_tiles),
        grid=(t // tm, n_tiles + 1),
        in_specs=[pl.BlockSpec((d // 2, tm), lambda i, j: (0, i)),
                  pl.BlockSpec((tm, d), lambda i, j: (i, 0)),
                  pl.BlockSpec((te // 2, d), lambda i, j: (jnp.minimum(j, n_tiles - 1), 0)),
                  pl.BlockSpec((d // 2, te), lambda i, j: (0, jnp.maximum(j - 1, 0))),
                  side_spec, side_spec, side16_spec, side16_spec],
        out_specs=pl.BlockSpec((tm, d), lambda i, j: (i, 0)),
        out_shape=jax.ShapeDtypeStruct((t, d), F32),
        scratch_shapes=[pltpu.VMEM((d, tm), F32),
                        pltpu.VMEM((half, tm), F32), pltpu.VMEM((half, tm), F32),
                        pltpu.VMEM((half // 2, tm), jnp.uint32),
                        pltpu.VMEM((half // 2, tm), jnp.uint32)],
        compiler_params=pltpu.CompilerParams(
            dimension_semantics=("arbitrary", "arbitrary"), vmem_limit_bytes=VMEM_LIMIT),
        name="peer",
    )(xn, h2d, u_bf, vt_bf, nrow, p1, rho, p2)


def _ple_kernel(h_ref, p_ref, g_ref, wpg_ref, wpe_ref, gfin_ref, o_ref, *, final):
    h = h_ref[...]
    gate = jax.nn.sigmoid(_dot(_rmsnorm(h, g_ref[...]).astype(BF16), wpg_ref[...]))
    out = h + _dot(p_ref[...].astype(BF16), wpe_ref[...]) * gate
    if final:
        out = _rmsnorm(out, gfin_ref[...])
    o_ref[...] = out


def _ple(h2d, p2d, g_ple, w_pg, w_pe, g_final, *, tm, final):
    t, d = h2d.shape
    dp = p2d.shape[1]
    return pl.pallas_call(
        functools.partial(_ple_kernel, final=final),
        grid=(t // tm,),
        in_specs=[pl.BlockSpec((tm, d), lambda i: (i, 0)),
                  pl.BlockSpec((tm, dp), lambda i: (i, 0)),
                  pl.BlockSpec((1, d), lambda i: (0, 0)),
                  pl.BlockSpec((d, d), lambda i: (0, 0)),
                  pl.BlockSpec((dp, d), lambda i: (0, 0)),
                  pl.BlockSpec((1, d), lambda i: (0, 0))],
        out_specs=pl.BlockSpec((tm, d), lambda i: (i, 0)),
        out_shape=jax.ShapeDtypeStruct((t, d), F32),
        compiler_params=pltpu.CompilerParams(
            dimension_semantics=("arbitrary",), vmem_limit_bytes=VMEM_LIMIT),
        name="ple_final" if final else "ple",
    )(h2d, p2d, g_ple.reshape(1, d), w_pg, w_pe, g_final.reshape(1, d))


def kernel(x, p, g_mix, w_in, conv_a_w, conv_a_b, ln_a_g, ln_a_b, conv_b_w, ln_c_g, ln_c_b, w_s, b_s, w_out, g_ffn, w_q, sub_keys, expert_u, expert_v, g_ple, w_pe, w_pg, g_final):
    depth = w_in.shape[0]
    bsz, seq, d = x.shape
    t = bsz * seq
    d_c = ln_c_g.shape[-1]
    heads_c = d_c // HEAD_DIM

    lane = jnp.arange(d_c)
    gavg = jnp.where(lane[:, None] // HEAD_DIM == lane[None, :] // HEAD_DIM, 1.0 / HEAD_DIM, 0.0).astype(BF16)
    pos = jnp.arange(SG_BLOCK)
    causal = (pos[None, :] // CHUNK) <= (pos[:, None] // CHUNK)

    h = x
    for i in range(depth):
        wss = jnp.where(causal[None], w_s[i], 0.0).astype(BF16).reshape(heads_c * SG_BLOCK, SG_BLOCK)
        bsx = jnp.repeat(b_s[i].T, HEAD_DIM, axis=1)
        h = _mixer(h, g_mix[i], w_in[i].astype(BF16), conv_a_w[i], conv_a_b[i], ln_a_g[i], ln_a_b[i],
                   conv_b_w[i], ln_c_g[i], ln_c_b[i], wss, bsx, gavg, w_out[i].astype(BF16), ts=512)
        h2d = h.reshape(t, d)
        n_heads, _, n_keys, dk = sub_keys[i].shape
        keys = _pack_rows(jnp.swapaxes(sub_keys[i], 0, 1).reshape(2 * n_heads * n_keys, dk))
        keys = keys.reshape(2, n_heads, n_keys // 2, dk)
        xn, nrow, p1, rho, p2 = _route(h2d, g_ffn[i], _pack_rows(w_q[i], transpose=True),
                                       keys[0], keys[1], tm=256)
        h2d = _peer(xn, h2d, _pack_rows(expert_u[i]), _pack_rows(expert_v[i], transpose=True),
                    nrow, p1, rho, p2, tm=512, te=1024)
        h2d = _ple(h2d, p[i].reshape(t, -1), g_ple[i], w_pg[i].astype(BF16), w_pe[i].astype(BF16),
                   g_final, tm=512, final=(i == depth - 1))
        h = h2d.reshape(bsz, seq, d)
    return h
```

```python
import functools

import jax
import jax.numpy as jnp
from jax import lax
from jax.experimental import pallas as pl
from jax.experimental.pallas import tpu as pltpu

EPS = 1e-6
HEAD_DIM = 64
CHUNK = 64
SG_BLOCK = 128
TOPK = 16
LANES = 128
SUBLANES = 8
HALO_A = 32
HALO_B = 8
VMEM_LIMIT = 56 * 1024 * 1024

F32 = jnp.float32
BF16 = jnp.bfloat16


def _dot(a, b):
    return jnp.dot(a, b, preferred_element_type=F32)


def _dot_nt(a, b):
    return lax.dot_general(a, b, (((1,), (1,)), ((), ())), preferred_element_type=F32)


def _pack_kernel(w_ref, o_ref, *, transpose):
    w = w_ref[...]
    if transpose:
        w = w.T
    o_ref[...] = pltpu.bitcast(w.astype(BF16), jnp.uint32)


def _pack_rows(w, *, transpose=False, rows=512):
    n, d = w.shape
    rows = min(rows, n)
    assert n % rows == 0 and rows % (2 * SUBLANES) == 0
    if transpose:
        in_spec = pl.BlockSpec((rows, d), lambda i: (i, 0))
        out_spec = pl.BlockSpec((d // 2, rows), lambda i: (0, i))
        out_shape = jax.ShapeDtypeStruct((d // 2, n), jnp.uint32)
    else:
        in_spec = pl.BlockSpec((rows, d), lambda i: (i, 0))
        out_spec = pl.BlockSpec((rows // 2, d), lambda i: (i, 0))
        out_shape = jax.ShapeDtypeStruct((n // 2, d), jnp.uint32)
    return pl.pallas_call(
        functools.partial(_pack_kernel, transpose=transpose),
        grid=(n // rows,), in_specs=[in_spec], out_specs=out_spec, out_shape=out_shape,
        compiler_params=pltpu.CompilerParams(
            dimension_semantics=("arbitrary",), vmem_limit_bytes=VMEM_LIMIT),
        name="pack_t" if transpose else "pack",
    )(w)


def _rmsnorm(x, g):
    return x * lax.rsqrt(jnp.mean(x * x, axis=-1, keepdims=True) + EPS) * g


def _dot_hilo(x, w):
    hi = x.astype(BF16)
    lo = (x - hi.astype(F32)).astype(BF16)
    return _dot(hi, w) + _dot(lo, w)


def _group_layernorm(x, gavg, g, b):
    mean = _dot_hilo(x, gavg)
    d = x - mean
    var = _dot_hilo(d * d, gavg)
    return d * lax.rsqrt(var + EPS) * g + b


def _gelu_tanh(x):
    c = 0.7978845608028654
    h = 0.5 * x
    return h + h * jnp.tanh(x * (c + (c * 0.044715) * (x * x)))


def _mixer_kernel(h_ref, gmix_ref, win_ref, caw_ref, cab_ref, lag_ref, lab_ref, cbw_ref,
                  lcg_ref, lcb_ref, wss_ref, bsx_ref, gavg_ref, wout_ref, o_ref,
                  ybuf, cbuf, ya_buf, *, d_a, d_b, d_c, ka, kb, ts, rows):
    @pl.when(pl.program_id(1) == 0)
    def _():
        ybuf[0:HALO_A, :] = jnp.zeros((HALO_A, d_a), F32)
        cbuf[0:HALO_B, :] = jnp.zeros((HALO_B, d_b), F32)

    h = h_ref[0]
    a = _rmsnorm(h, gmix_ref[...]).astype(BF16)
    z = _dot(a, win_ref[...])
    o_b = 2 * d_a
    o_c = o_b + 3 * d_b

    ybuf[HALO_A:HALO_A + ts, :] = z[:, :d_a] * jax.nn.sigmoid(z[:, d_a:2 * d_a])
    for r0 in range(0, ts, rows):
        acc = jnp.zeros((rows, d_a), F32) + cab_ref[...]
        for k in range(ka):
            off = HALO_A - (ka - 1) + k + r0
            acc = acc + ybuf[off:off + rows, :] * caw_ref[k:k + 1, :]
        ya_buf[r0:r0 + rows, :] = acc
    ybuf[0:HALO_A, :] = ybuf[ts:ts + HALO_A, :]
    ya = _group_layernorm(ya_buf[...], gavg_ref[...], lag_ref[...], lab_ref[...])
    ya = ya * jax.nn.sigmoid(ya)

    cbuf[HALO_B:HALO_B + ts, :] = z[:, o_b + d_b:o_b + 2 * d_b] * z[:, o_b + 2 * d_b:o_c]
    cb = jnp.zeros((ts, d_b), F32)
    for k in range(kb):
        off = HALO_B - (kb - 1) + k
        cb = cb + cbuf[off:off + ts, :] * cbw_ref[k:k + 1, :]
    cbuf[0:HALO_B, :] = cbuf[ts:ts + HALO_B, :]
    yb = z[:, o_b:o_b + d_b] * cb

    u = z[:, o_c:o_c + d_c]
    vln = _group_layernorm(z[:, o_c + d_c:], gavg_ref[...], lcg_ref[...], lcb_ref[...]).astype(BF16)
    heads_c = d_c // HEAD_DIM
    lane_head = lax.broadcasted_iota(jnp.int32, (SG_BLOCK, d_c), 1) // HEAD_DIM
    yc_blocks = []
    for n in range(ts // SG_BLOCK):
        sl = slice(n * SG_BLOCK, (n + 1) * SG_BLOCK)
        r = _dot(wss_ref[...], vln[sl, :])
        mixed = bsx_ref[...]
        for hd in range(heads_c):
            mixed = mixed + jnp.where(lane_head == hd, r[hd * SG_BLOCK:(hd + 1) * SG_BLOCK, :], 0.0)
        yc_blocks.append(u[sl, :] * mixed)
    yc = jnp.concatenate(yc_blocks, axis=0)

    ycat = jnp.concatenate([ya, yb, yc], axis=1).astype(BF16)
    o_ref[0] = h + _dot(ycat, wout_ref[...])


def _mixer(h, g_mix, w_in, conv_a_w, conv_a_b, ln_a_g, ln_a_b, conv_b_w, ln_c_g, ln_c_b,
           wss, bsx, gavg, w_out, *, ts):
    bsz, seq, d = h.shape
    ka, d_a = conv_a_w.shape
    kb, d_b = conv_b_w.shape
    d_c = ln_c_g.shape[-1]
    assert d_a == d_c and seq % ts == 0 and ts % SG_BLOCK == 0
    row = lambda v: v.reshape(1, -1)
    full = lambda arr: pl.BlockSpec(arr.shape, lambda b, s: (0,) * arr.ndim)
    operands = (row(g_mix), w_in, conv_a_w, row(conv_a_b), row(ln_a_g), row(ln_a_b), conv_b_w,
                row(ln_c_g), row(ln_c_b), wss, bsx, gavg, w_out)
    blk = pl.BlockSpec((1, ts, d), lambda b, s: (b, s, 0))
    return pl.pallas_call(
        functools.partial(_mixer_kernel, d_a=d_a, d_b=d_b, d_c=d_c, ka=ka, kb=kb, ts=ts, rows=64),
        grid=(bsz, seq // ts),
        in_specs=[blk] + [full(o) for o in operands],
        out_specs=blk,
        out_shape=jax.ShapeDtypeStruct(h.shape, F32),
        scratch_shapes=[pltpu.VMEM((ts + HALO_A, d_a), F32),
                        pltpu.VMEM((ts + HALO_B, d_b), F32),
                        pltpu.VMEM((ts, d_a), F32)],
        compiler_params=pltpu.CompilerParams(
            dimension_semantics=("arbitrary", "arbitrary"), vmem_limit_bytes=VMEM_LIMIT),
        name="mixer",
    )(h, *operands)


def _merge_exchange_pairs(n):
    pairs = []
    t = (n - 1).bit_length()
    p = 1 << (t - 1)
    while p > 0:
        q, r, d = 1 << (t - 1), 0, p
        while True:
            pairs.extend((i, i + d) for i in range(n - d) if (i & p) == r)
            if q == p:
                break
            d, q, r = q - p, q >> 1, p
        p >>= 1
    return pairs


def _top_values(s, k):
    n = s.shape[0] // SUBLANES
    lists = [s[i * SUBLANES:(i + 1) * SUBLANES, :] for i in range(n)]
    for i, j in _merge_exchange_pairs(n):
        lists[i], lists[j] = jnp.maximum(lists[i], lists[j]), jnp.minimum(lists[i], lists[j])
    vals = []
    for r in range(k):
        m = jnp.max(lists[0], axis=0, keepdims=True)
        vals.append(m)
        if r + 1 < k:
            head = lists[0] == m
            for i in range(min(n - 1, k - 1 - r)):
                lists[i] = jnp.where(head, lists[i + 1], lists[i])
            if n - 1 < k - 1 - r:
                lists[n - 1] = jnp.where(head, -jnp.inf, lists[n - 1])
    return jnp.concatenate(vals, axis=0)


def _pair_bf16(x):
    b = pltpu.bitcast(x.astype(BF16).astype(F32), jnp.uint32)
    return b | (b >> 16)


def _route_kernel(h_ref, g_ref, wqt_ref, k1_ref, k2_ref,
                  xn_ref, np_ref, rp_ref, q_scr, *, heads, dk, tm):
    xnt = _rmsnorm(h_ref[...], g_ref[...]).T.astype(BF16)
    xn_ref[...] = pltpu.bitcast(xnt, jnp.uint32)
    q_scr[...] = _dot(pltpu.bitcast(wqt_ref[...], BF16), xnt)

    def head_body(hd, carry):
        base = pl.multiple_of(hd * (2 * dk), 2 * dk)
        q1 = q_scr[pl.ds(base, dk), :].astype(BF16)
        q2 = q_scr[pl.ds(base + dk, dk), :].astype(BF16)
        s1 = _dot(pltpu.bitcast(k1_ref[hd], BF16), q1)
        s2 = _dot(pltpu.bitcast(k2_ref[hd], BF16), q2)
        for c in range(tm // LANES):
            cs = slice(c * LANES, (c + 1) * LANES)
            a1 = s1[:, cs]
            a2 = s2[:, cs]
            v1 = _top_values(a1, TOPK)
            v2 = _top_values(a2, TOPK)
            rho2 = jnp.full(a2.shape, float(TOPK), F32)
            for r in reversed(range(TOPK)):
                rho2 = jnp.where(a2 >= v2[r:r + 1, :], float(r), rho2)
            hs = TOPK // 2
            cand = jnp.concatenate(
                [v1[0:1, :] + v2]
                + [v1[a:a + 1, :] + v2[0:hs, :] for a in range(1, hs)]
                + [v1[hs:, :] + v2[0:1, :]], axis=0)
            tv = _top_values(cand, TOPK)
            tau = tv[TOPK - 1:TOPK, :]
            cmax = tv[0:1, :]
            z = jnp.sum(jnp.where(cand >= tau, jnp.exp(cand - cmax), 0.0), axis=0, keepdims=True)
            top_hi = jnp.sum(jnp.where(v1[0:1, :] + v2[hs:, :] >= tau, 1.0, 0.0), axis=0, keepdims=True)
            nrow = jnp.where(a1 == v1[0:1, :], top_hi, 0.0)
            for b in range(hs):
                nrow = nrow + jnp.where(a1 + v2[b:b + 1, :] >= tau, 1.0, 0.0)
            rows = (_pair_bf16(nrow), _pair_bf16(jnp.exp(a1 - v1[0:1, :]) / z))
            tiles = (pltpu.bitcast(rho2.astype(BF16), jnp.uint32),
                     pltpu.bitcast(jnp.exp(a2 - v2[0:1, :]).astype(BF16), jnp.uint32))
            for which in range(2):
                for g in range(rows[which].shape[0] // SUBLANES):
                    np_ref[c, g, hd, which] = rows[which][g * SUBLANES:(g + 1) * SUBLANES, :]
                for m in range(tiles[which].shape[0] // SUBLANES):
                    rp_ref[c, m, hd, which] = tiles[which][m * SUBLANES:(m + 1) * SUBLANES, :]
        return carry

    lax.fori_loop(0, heads, head_body, 0)


def _route(h2d, g_ffn, wqt, k1, k2, *, tm):
    t, d = h2d.shape
    heads, nk, dk = k1.shape[0], 2 * k1.shape[1], k1.shape[2]
    assert t % tm == 0 and tm % LANES == 0
    np_dims = (nk // SUBLANES, heads, 2, SUBLANES, LANES)
    rp_dims = (nk // (2 * SUBLANES), heads, 2, SUBLANES, LANES)
    side_spec = lambda dims: pl.BlockSpec((tm // LANES,) + dims, lambda i: (i, 0, 0, 0, 0, 0))
    side_shape = lambda dims: jax.ShapeDtypeStruct((t // LANES,) + dims, jnp.uint32)
    return pl.pallas_call(
        functools.partial(_route_kernel, heads=heads, dk=dk, tm=tm),
        grid=(t // tm,),
        in_specs=[pl.BlockSpec((tm, d), lambda i: (i, 0)),
                  pl.BlockSpec((1, d), lambda i: (0, 0)),
                  pl.BlockSpec(wqt.shape, lambda i: (0, 0)),
                  pl.BlockSpec(k1.shape, lambda i: (0, 0, 0)),
                  pl.BlockSpec(k2.shape, lambda i: (0, 0, 0))],
        out_specs=[pl.BlockSpec((d // 2, tm), lambda i: (0, i)), side_spec(np_dims), side_spec(rp_dims)],
        out_shape=[jax.ShapeDtypeStruct((d // 2, t), jnp.uint32), side_shape(np_dims), side_shape(rp_dims)],
        scratch_shapes=[pltpu.VMEM((2 * wqt.shape[0], tm), F32)],
        compiler_params=pltpu.CompilerParams(
            dimension_semantics=("arbitrary",), vmem_limit_bytes=VMEM_LIMIT),
        name="route",
    )(h2d, g_ffn.reshape(1, d), wqt, k1, k2)


def _peer_kernel(xn_ref, h_ref, u_ref, vt_ref, np_ref, rp_ref, o_ref,
                 acc, s_a, s_b, act_a, act_b, *, heads, nk, tm, te, n_tiles):
    j = pl.program_id(1)
    half = te // 2
    rows_half = half // nk
    pack = 2 * SUBLANES

    def scores(hf, s_buf):
        u_half = pltpu.bitcast(u_ref[hf * half // 2:(hf + 1) * half // 2, :], BF16)
        s = _dot(u_half, pltpu.bitcast(xn_ref[...], BF16))
        for c in range(tm // LANES):
            s_buf[c] = s[:, c * LANES:(c + 1) * LANES]

    def accumulate(act_buf, hf):
        vt_half = pltpu.bitcast(vt_ref[:, hf * half:(hf + 1) * half], BF16)
        act = jnp.concatenate([pltpu.bitcast(act_buf[c], BF16) for c in range(tm // LANES)], axis=1)
        acc[...] += _dot(vt_half, act)

    def gate(s_buf, act_buf, tile, hf):
        for c in range(tm // LANES):
            nr8 = [np_ref[c, tile, hd, 0] for hd in range(heads)]
            pr8 = [np_ref[c, tile, hd, 1] for hd in range(heads)]
            for k in range(rows_half):
                kk = hf * rows_half + k
                nr = [pltpu.bitcast(jnp.broadcast_to(nr8[hd][kk:kk + 1, :], (SUBLANES, LANES)), BF16)
                      for hd in range(heads)]
                pr = [pltpu.bitcast(jnp.broadcast_to(pr8[hd][kk:kk + 1, :], (SUBLANES, LANES)), BF16)
                      for hd in range(heads)]
                for m in range(nk // pack):
                    w = None
                    for hd in range(heads):
                        keep = pltpu.bitcast(rp_ref[c, m, hd, 0], BF16) < nr[hd]
                        p2 = pltpu.bitcast(rp_ref[c, m, hd, 1], BF16)
                        term = jnp.where(keep, p2, jnp.zeros((), BF16)) * pr[hd]
                        w = term if w is None else w + term
                    rs = slice(k * nk + m * pack, k * nk + (m + 1) * pack)
                    act = _gelu_tanh(s_buf[c, rs, :].astype(BF16)) * w
                    ws = slice((k * nk + m * pack) // 2, (k * nk + (m + 1) * pack) // 2)
                    act_buf[c, ws, :] = pltpu.bitcast(act, jnp.uint32)

    steady = jnp.logical_and(j > 0, j < n_tiles)

    @pl.when(j == 0)
    def _():
        acc[...] = jnp.zeros(acc.shape, F32)
        scores(0, s_a)

    @pl.when(j == 0)
    def _():
        scores(1, s_b)
        gate(s_a, act_a, j, 0)

    @pl.when(steady)
    def _():
        gate(s_b, act_b, j - 1, 1)
        scores(0, s_a)
        accumulate(act_a, 0)

    @pl.when(steady)
    def _():
        gate(s_a, act_a, j, 0)
        scores(1, s_b)
        accumulate(act_b, 1)

    @pl.when(j == n_tiles)
    def _():
        gate(s_b, act_b, j - 1, 1)
        accumulate(act_a, 0)

    @pl.when(j == n_tiles)
    def _():
        accumulate(act_b, 1)
        o_ref[...] = h_ref[...] + acc[...].T


def _peer(xn, h2d, u_bf, vt_bf, np_side, rp_side, *, tm, te):
    t, d = h2d.shape
    n_exp = 2 * u_bf.shape[0]
    nk, heads = np_side.shape[1] * SUBLANES, np_side.shape[2]
    assert te == nk * SUBLANES and n_exp == nk * nk and t % tm == 0 and tm % LANES == 0
    n_tiles = n_exp // te
    half = te // 2
    side_spec = lambda a: pl.BlockSpec((tm // LANES,) + a.shape[1:], lambda i, j: (i, 0, 0, 0, 0, 0))
    return pl.pallas_call(
        functools.partial(_peer_kernel, heads=heads, nk=nk, tm=tm, te=te, n_tiles=n_tiles),
        grid=(t // tm, n_tiles + 1),
        in_specs=[pl.BlockSpec((d // 2, tm), lambda i, j: (0, i)),
                  pl.BlockSpec((tm, d), lambda i, j: (i, 0)),
                  pl.BlockSpec((te // 2, d), lambda i, j: (jnp.minimum(j, n_tiles - 1), 0)),
                  pl.BlockSpec((d // 2, te), lambda i, j: (0, jnp.maximum(j - 1, 0))),
                  side_spec(np_side), side_spec(rp_side)],
        out_specs=pl.BlockSpec((tm, d), lambda i, j: (i, 0)),
        out_shape=jax.ShapeDtypeStruct((t, d), F32),
        scratch_shapes=[pltpu.VMEM((d, tm), F32),
                        pltpu.VMEM((tm // LANES, half, LANES), F32),
                        pltpu.VMEM((tm // LANES, half, LANES), F32),
                        pltpu.VMEM((tm // LANES, half // 2, LANES), jnp.uint32),
                        pltpu.VMEM((tm // LANES, half // 2, LANES), jnp.uint32)],
        compiler_params=pltpu.CompilerParams(
            dimension_semantics=("arbitrary", "arbitrary"), vmem_limit_bytes=VMEM_LIMIT),
        name="peer",
    )(xn, h2d, u_bf, vt_bf, np_side, rp_side)


def _ple_kernel(h_ref, p_ref, g_ref, wpg_ref, wpe_ref, gfin_ref, o_ref, *, final):
    h = h_ref[...]
    gate = jax.nn.sigmoid(_dot(_rmsnorm(h, g_ref[...]).astype(BF16), wpg_ref[...]))
    out = h + _dot(p_ref[...].astype(BF16), wpe_ref[...]) * gate
    if final:
        out = _rmsnorm(out, gfin_ref[...])
    o_ref[...] = out


def _ple(h2d, p2d, g_ple, w_pg, w_pe, g_final, *, tm, final):
    t, d = h2d.shape
    dp = p2d.shape[1]
    return pl.pallas_call(
        functools.partial(_ple_kernel, final=final),
        grid=(t // tm,),
        in_specs=[pl.BlockSpec((tm, d), lambda i: (i, 0)),
                  pl.BlockSpec((tm, dp), lambda i: (i, 0)),
                  pl.BlockSpec((1, d), lambda i: (0, 0)),
                  pl.BlockSpec((d, d), lambda i: (0, 0)),
                  pl.BlockSpec((dp, d), lambda i: (0, 0)),
                  pl.BlockSpec((1, d), lambda i: (0, 0))],
        out_specs=pl.BlockSpec((tm, d), lambda i: (i, 0)),
        out_shape=jax.ShapeDtypeStruct((t, d), F32),
        compiler_params=pltpu.CompilerParams(
            dimension_semantics=("arbitrary",), vmem_limit_bytes=VMEM_LIMIT),
        name="ple_final" if final else "ple",
    )(h2d, p2d, g_ple.reshape(1, d), w_pg, w_pe, g_final.reshape(1, d))


def kernel(x, p, g_mix, w_in, conv_a_w, conv_a_b, ln_a_g, ln_a_b, conv_b_w, ln_c_g, ln_c_b, w_s, b_s, w_out, g_ffn, w_q, sub_keys, expert_u, expert_v, g_ple, w_pe, w_pg, g_final):
    depth = w_in.shape[0]
    bsz, seq, d = x.shape
    t = bsz * seq
    d_c = ln_c_g.shape[-1]
    heads_c = d_c // HEAD_DIM

    lane = jnp.arange(d_c)
    gavg = jnp.where(lane[:, None] // HEAD_DIM == lane[None, :] // HEAD_DIM, 1.0 / HEAD_DIM, 0.0).astype(BF16)
    pos = jnp.arange(SG_BLOCK)
    causal = (pos[None, :] // CHUNK) <= (pos[:, None] // CHUNK)

    h = x
    for i in range(depth):
        wss = jnp.where(causal[None], w_s[i], 0.0).astype(BF16).reshape(heads_c * SG_BLOCK, SG_BLOCK)
        bsx = jnp.repeat(b_s[i].T, HEAD_DIM, axis=1)
        h = _mixer(h, g_mix[i], w_in[i].astype(BF16), conv_a_w[i], conv_a_b[i], ln_a_g[i], ln_a_b[i],
                   conv_b_w[i], ln_c_g[i], ln_c_b[i], wss, bsx, gavg, w_out[i].astype(BF16), ts=512)
        h2d = h.reshape(t, d)
        n_heads, _, n_keys, dk = sub_keys[i].shape
        keys = _pack_rows(jnp.swapaxes(sub_keys[i], 0, 1).reshape(2 * n_heads * n_keys, dk))
        keys = keys.reshape(2, n_heads, n_keys // 2, dk)
        xn, np_side, rp_side = _route(h2d, g_ffn[i], _pack_rows(w_q[i], transpose=True),
                                       keys[0], keys[1], tm=256)
        h2d = _peer(xn, h2d, _pack_rows(expert_u[i]), _pack_rows(expert_v[i], transpose=True),
                    np_side, rp_side, tm=512, te=1024)
        h2d = _ple(h2d, p[i].reshape(t, -1), g_ple[i], w_pg[i].astype(BF16), w_pe[i].astype(BF16),
                   g_final, tm=512, final=(i == depth - 1))
        h = h2d.reshape(bsz, seq, d)
    return h
```

```python
import functools

import jax
import jax.numpy as jnp
from jax import lax
from jax.experimental import pallas as pl
from jax.experimental.pallas import tpu as pltpu

EPS = 1e-6
HEAD_DIM = 64
CHUNK = 64
SG_BLOCK = 128
TOPK = 16
LANES = 128
SUBLANES = 8
HALO_A = 32
HALO_B = 8
VMEM_LIMIT = 56 * 1024 * 1024

F32 = jnp.float32
BF16 = jnp.bfloat16


def _dot(a, b):
    return jnp.dot(a, b, preferred_element_type=F32)


def _dot_nt(a, b):
    return lax.dot_general(a, b, (((1,), (1,)), ((), ())), preferred_element_type=F32)


def _pack_kernel(w_ref, o_ref, *, transpose):
    w = w_ref[...]
    if transpose:
        w = w.T
    o_ref[...] = pltpu.bitcast(w.astype(BF16), jnp.uint32)


def _pack_rows(w, *, transpose=False, rows=512):
    n, d = w.shape
    rows = min(rows, n)
    assert n % rows == 0 and rows % (2 * SUBLANES) == 0
    if transpose:
        in_spec = pl.BlockSpec((rows, d), lambda i: (i, 0))
        out_spec = pl.BlockSpec((d // 2, rows), lambda i: (0, i))
        out_shape = jax.ShapeDtypeStruct((d // 2, n), jnp.uint32)
    else:
        in_spec = pl.BlockSpec((rows, d), lambda i: (i, 0))
        out_spec = pl.BlockSpec((rows // 2, d), lambda i: (i, 0))
        out_shape = jax.ShapeDtypeStruct((n // 2, d), jnp.uint32)
    return pl.pallas_call(
        functools.partial(_pack_kernel, transpose=transpose),
        grid=(n // rows,), in_specs=[in_spec], out_specs=out_spec, out_shape=out_shape,
        compiler_params=pltpu.CompilerParams(
            dimension_semantics=("arbitrary",), vmem_limit_bytes=VMEM_LIMIT),
        name="pack_t" if transpose else "pack",
    )(w)


def _rmsnorm(x, g):
    return x * lax.rsqrt(jnp.mean(x * x, axis=-1, keepdims=True) + EPS) * g


def _dot_hilo(x, w):
    hi = x.astype(BF16)
    lo = (x - hi.astype(F32)).astype(BF16)
    return _dot(hi, w) + _dot(lo, w)


def _group_layernorm(x, gavg, g, b):
    mean = _dot_hilo(x, gavg)
    d = x - mean
    var = _dot_hilo(d * d, gavg)
    return d * lax.rsqrt(var + EPS) * g + b


def _gelu_tanh(x):
    c = 0.7978845608028654
    h = 0.5 * x
    return h + h * jnp.tanh(x * (c + (c * 0.044715) * (x * x)))


def _mixer_kernel(h_ref, gmix_ref, win_ref, caw_ref, cab_ref, lag_ref, lab_ref, cbw_ref,
                  lcg_ref, lcb_ref, wss_ref, bsx_ref, gavg_ref, wout_ref, o_ref,
                  ybuf, cbuf, ya_buf, *, d_a, d_b, d_c, ka, kb, ts, rows):
    @pl.when(pl.program_id(1) == 0)
    def _():
        ybuf[0:HALO_A, :] = jnp.zeros((HALO_A, d_a), F32)
        cbuf[0:HALO_B, :] = jnp.zeros((HALO_B, d_b), F32)

    h = h_ref[0]
    a = _rmsnorm(h, gmix_ref[...]).astype(BF16)
    z = _dot(a, win_ref[...])
    o_b = 2 * d_a
    o_c = o_b + 3 * d_b

    ybuf[HALO_A:HALO_A + ts, :] = z[:, :d_a] * jax.nn.sigmoid(z[:, d_a:2 * d_a])
    for r0 in range(0, ts, rows):
        acc = jnp.zeros((rows, d_a), F32) + cab_ref[...]
        for k in range(ka):
            off = HALO_A - (ka - 1) + k + r0
            acc = acc + ybuf[off:off + rows, :] * caw_ref[k:k + 1, :]
        ya_buf[r0:r0 + rows, :] = acc
    ybuf[0:HALO_A, :] = ybuf[ts:ts + HALO_A, :]
    ya = _group_layernorm(ya_buf[...], gavg_ref[...], lag_ref[...], lab_ref[...])
    ya = ya * jax.nn.sigmoid(ya)

    cbuf[HALO_B:HALO_B + ts, :] = z[:, o_b + d_b:o_b + 2 * d_b] * z[:, o_b + 2 * d_b:o_c]
    cb = jnp.zeros((ts, d_b), F32)
    for k in range(kb):
        off = HALO_B - (kb - 1) + k
        cb = cb + cbuf[off:off + ts, :] * cbw_ref[k:k + 1, :]
    cbuf[0:HALO_B, :] = cbuf[ts:ts + HALO_B, :]
    yb = z[:, o_b:o_b + d_b] * cb

    u = z[:, o_c:o_c + d_c]
    vln = _group_layernorm(z[:, o_c + d_c:], gavg_ref[...], lcg_ref[...], lcb_ref[...]).astype(BF16)
    heads_c = d_c // HEAD_DIM
    lane_head = lax.broadcasted_iota(jnp.int32, (SG_BLOCK, d_c), 1) // HEAD_DIM
    yc_blocks = []
    for n in range(ts // SG_BLOCK):
        sl = slice(n * SG_BLOCK, (n + 1) * SG_BLOCK)
        r = _dot(wss_ref[...], vln[sl, :])
        mixed = bsx_ref[...]
        for hd in range(heads_c):
            mixed = mixed + jnp.where(lane_head == hd, r[hd * SG_BLOCK:(hd + 1) * SG_BLOCK, :], 0.0)
        yc_blocks.append(u[sl, :] * mixed)
    yc = jnp.concatenate(yc_blocks, axis=0)

    ycat = jnp.concatenate([ya, yb, yc], axis=1).astype(BF16)
    o_ref[0] = h + _dot(ycat, wout_ref[...])


def _mixer(h, g_mix, w_in, conv_a_w, conv_a_b, ln_a_g, ln_a_b, conv_b_w, ln_c_g, ln_c_b,
           wss, bsx, gavg, w_out, *, ts):
    bsz, seq, d = h.shape
    ka, d_a = conv_a_w.shape
    kb, d_b = conv_b_w.shape
    d_c = ln_c_g.shape[-1]
    assert d_a == d_c and seq % ts == 0 and ts % SG_BLOCK == 0
    row = lambda v: v.reshape(1, -1)
    full = lambda arr: pl.BlockSpec(arr.shape, lambda b, s: (0,) * arr.ndim)
    operands = (row(g_mix), w_in, conv_a_w, row(conv_a_b), row(ln_a_g), row(ln_a_b), conv_b_w,
                row(ln_c_g), row(ln_c_b), wss, bsx, gavg, w_out)
    blk = pl.BlockSpec((1, ts, d), lambda b, s: (b, s, 0))
    return pl.pallas_call(
        functools.partial(_mixer_kernel, d_a=d_a, d_b=d_b, d_c=d_c, ka=ka, kb=kb, ts=ts, rows=64),
        grid=(bsz, seq // ts),
        in_specs=[blk] + [full(o) for o in operands],
        out_specs=blk,
        out_shape=jax.ShapeDtypeStruct(h.shape, F32),
        scratch_shapes=[pltpu.VMEM((ts + HALO_A, d_a), F32),
                        pltpu.VMEM((ts + HALO_B, d_b), F32),
                        pltpu.VMEM((ts, d_a), F32)],
        compiler_params=pltpu.CompilerParams(
            dimension_semantics=("arbitrary", "arbitrary"), vmem_limit_bytes=VMEM_LIMIT),
        name="mixer",
    )(h, *operands)


def _merge_exchange_pairs(n):
    pairs = []
    t = (n - 1).bit_length()
    p = 1 << (t - 1)
    while p > 0:
        q, r, d = 1 << (t - 1), 0, p
        while True:
            pairs.extend((i, i + d) for i in range(n - d) if (i & p) == r)
            if q == p:
                break
            d, q, r = q - p, q >> 1, p
        p >>= 1
    return pairs


def _top_values(s, k):
    n = s.shape[0] // SUBLANES
    lists = [s[i * SUBLANES:(i + 1) * SUBLANES, :] for i in range(n)]
    for i, j in _merge_exchange_pairs(n):
        lists[i], lists[j] = jnp.maximum(lists[i], lists[j]), jnp.minimum(lists[i], lists[j])
    vals = []
    for r in range(k):
        m = jnp.max(lists[0], axis=0, keepdims=True)
        vals.append(m)
        if r + 1 < k:
            head = lists[0] == m
            for i in range(min(n - 1, k - 1 - r)):
                lists[i] = jnp.where(head, lists[i + 1], lists[i])
            if n - 1 < k - 1 - r:
                lists[n - 1] = jnp.where(head, -jnp.inf, lists[n - 1])
    return jnp.concatenate(vals, axis=0)


def _pair_bf16(x):
    b = pltpu.bitcast(x.astype(BF16).astype(F32), jnp.uint32)
    return b | (b >> 16)


def _route_kernel(h_ref, g_ref, wqt_ref, k1_ref, k2_ref,
                  xn_ref, np_ref, rp_ref, q_scr, *, heads, dk, tm):
    xnt = _rmsnorm(h_ref[...], g_ref[...]).T.astype(BF16)
    xn_ref[...] = pltpu.bitcast(xnt, jnp.uint32)
    q_scr[...] = _dot(pltpu.bitcast(wqt_ref[...], BF16), xnt)

    def head_body(hd, carry):
        base = pl.multiple_of(hd * (2 * dk), 2 * dk)
        q1 = q_scr[pl.ds(base, dk), :].astype(BF16)
        q2 = q_scr[pl.ds(base + dk, dk), :].astype(BF16)
        s1 = _dot(pltpu.bitcast(k1_ref[hd], BF16), q1)
        s2 = _dot(pltpu.bitcast(k2_ref[hd], BF16), q2)
        for c in range(tm // LANES):
            cs = slice(c * LANES, (c + 1) * LANES)
            a1 = s1[:, cs]
            a2 = s2[:, cs]
            v1 = _top_values(a1, TOPK)
            v2 = _top_values(a2, TOPK)
            rho2 = jnp.full(a2.shape, float(TOPK), F32)
            for r in reversed(range(TOPK)):
                rho2 = jnp.where(a2 >= v2[r:r + 1, :], float(r), rho2)
            hs = TOPK // 2
            cand = jnp.concatenate(
                [v1[0:1, :] + v2]
                + [v1[a:a + 1, :] + v2[0:hs, :] for a in range(1, hs)]
                + [v1[hs:, :] + v2[0:1, :]], axis=0)
            tv = _top_values(cand, TOPK)
            tau = tv[TOPK - 1:TOPK, :]
            cmax = tv[0:1, :]
            z = jnp.sum(jnp.where(cand >= tau, jnp.exp(cand - cmax), 0.0), axis=0, keepdims=True)
            top_hi = jnp.sum(jnp.where(v1[0:1, :] + v2[hs:, :] >= tau, 1.0, 0.0), axis=0, keepdims=True)
            nrow = jnp.where(a1 == v1[0:1, :], top_hi, 0.0)
            for b in range(hs):
                nrow = nrow + jnp.where(a1 + v2[b:b + 1, :] >= tau, 1.0, 0.0)
            rows = (_pair_bf16(nrow), _pair_bf16(jnp.exp(a1 - v1[0:1, :]) / z))
            tiles = (pltpu.bitcast(rho2.astype(BF16), jnp.uint32),
                     pltpu.bitcast(jnp.exp(a2 - v2[0:1, :]).astype(BF16), jnp.uint32))
            for which in range(2):
                for g in range(rows[which].shape[0] // SUBLANES):
                    np_ref[c, g, hd, which] = rows[which][g * SUBLANES:(g + 1) * SUBLANES, :]
                for m in range(tiles[which].shape[0] // SUBLANES):
                    rp_ref[c, m, hd, which] = tiles[which][m * SUBLANES:(m + 1) * SUBLANES, :]
        return carry

    lax.fori_loop(0, heads, head_body, 0)


def _route(h2d, g_ffn, wqt, k1, k2, *, tm):
    t, d = h2d.shape
    heads, nk, dk = k1.shape[0], 2 * k1.shape[1], k1.shape[2]
    assert t % tm == 0 and tm % LANES == 0
    np_dims = (nk // SUBLANES, heads, 2, SUBLANES, LANES)
    rp_dims = (nk // (2 * SUBLANES), heads, 2, SUBLANES, LANES)
    side_spec = lambda dims: pl.BlockSpec((tm // LANES,) + dims, lambda i: (i, 0, 0, 0, 0, 0))
    side_shape = lambda dims: jax.ShapeDtypeStruct((t // LANES,) + dims, jnp.uint32)
    return pl.pallas_call(
        functools.partial(_route_kernel, heads=heads, dk=dk, tm=tm),
        grid=(t // tm,),
        in_specs=[pl.BlockSpec((tm, d), lambda i: (i, 0)),
                  pl.BlockSpec((1, d), lambda i: (0, 0)),
                  pl.BlockSpec(wqt.shape, lambda i: (0, 0)),
                  pl.BlockSpec(k1.shape, lambda i: (0, 0, 0)),
                  pl.BlockSpec(k2.shape, lambda i: (0, 0, 0))],
        out_specs=[pl.BlockSpec((d // 2, tm), lambda i: (0, i)), side_spec(np_dims), side_spec(rp_dims)],
        out_shape=[jax.ShapeDtypeStruct((d // 2, t), jnp.uint32), side_shape(np_dims), side_shape(rp_dims)],
        scratch_shapes=[pltpu.VMEM((2 * wqt.shape[0], tm), F32)],
        compiler_params=pltpu.CompilerParams(
            dimension_semantics=("arbitrary",), vmem_limit_bytes=VMEM_LIMIT),
        name="route",
    )(h2d, g_ffn.reshape(1, d), wqt, k1, k2)


def _peer_kernel(xn_ref, h_ref, u_ref, vt_ref, np_ref, rp_ref, o_ref,
                 acc, s_a, s_b, act_a, act_b, *, heads, nk, tm, te, n_tiles):
    j = pl.program_id(1)
    half = te // 2
    rows_half = half // nk
    pack = 2 * SUBLANES

    def scores(hf, s_buf):
        u_half = pltpu.bitcast(u_ref[hf * half // 2:(hf + 1) * half // 2, :], BF16)
        s = _dot(u_half, pltpu.bitcast(xn_ref[...], BF16))
        for c in range(tm // LANES):
            s_buf[c] = s[:, c * LANES:(c + 1) * LANES]

    def accumulate(act_buf, hf):
        vt_half = pltpu.bitcast(vt_ref[:, hf * half:(hf + 1) * half], BF16)
        act = jnp.concatenate([pltpu.bitcast(act_buf[c], BF16) for c in range(tm // LANES)], axis=1)
        acc[...] += _dot(vt_half, act)

    def gate(s_buf, act_buf, tile, hf):
        for c in range(tm // LANES):
            nr8 = [np_ref[c, tile, hd, 0] for hd in range(heads)]
            pr8 = [np_ref[c, tile, hd, 1] for hd in range(heads)]
            for k in range(rows_half):
                kk = hf * rows_half + k
                nr = [pltpu.bitcast(jnp.broadcast_to(nr8[hd][kk:kk + 1, :], (SUBLANES, LANES)), BF16)
                      for hd in range(heads)]
                pr = [pltpu.bitcast(jnp.broadcast_to(pr8[hd][kk:kk + 1, :], (SUBLANES, LANES)), BF16)
                      for hd in range(heads)]
                for m in range(nk // pack):
                    w = None
                    for hd in range(heads):
                        keep = pltpu.bitcast(rp_ref[c, m, hd, 0], BF16) < nr[hd]
                        p2 = pltpu.bitcast(rp_ref[c, m, hd, 1], BF16)
                        term = jnp.where(keep, p2, jnp.zeros((), BF16)) * pr[hd]
                        w = term if w is None else w + term
                    rs = slice(k * nk + m * pack, k * nk + (m + 1) * pack)
                    act = _gelu_tanh(s_buf[c, rs, :].astype(BF16)) * w
                    ws = slice((k * nk + m * pack) // 2, (k * nk + (m + 1) * pack) // 2)
                    act_buf[c, ws, :] = pltpu.bitcast(act, jnp.uint32)

    first = j == 0
    steady, steady_b = jnp.logical_and(j > 0, j < n_tiles), jnp.logical_and(j >= 1, j <= n_tiles - 1)
    last = j == n_tiles

    @pl.when(first)
    def _():
        acc[...] = jnp.zeros(acc.shape, F32)
        scores(0, s_a)

    @pl.when(first)
    def _():
        scores(1, s_b)
        gate(s_a, act_a, j, 0)

    @pl.when(steady)
    def _():
        gate(s_b, act_b, j - 1, 1)
        scores(0, s_a)
        accumulate(act_a, 0)

    @pl.when(steady_b)
    def _():
        gate(s_a, act_a, j, 0)
        scores(1, s_b)
        accumulate(act_b, 1)

    @pl.when(last)
    def _():
        gate(s_b, act_b, j - 1, 1)
        accumulate(act_a, 0)

    @pl.when(last)
    def _():
        accumulate(act_b, 1)
        o_ref[...] = h_ref[...] + acc[...].T


def _peer(xn, h2d, u_bf, vt_bf, np_side, rp_side, *, tm, te):
    t, d = h2d.shape
    n_exp = 2 * u_bf.shape[0]
    nk, heads = np_side.shape[1] * SUBLANES, np_side.shape[2]
    assert te == nk * SUBLANES and n_exp == nk * nk and t % tm == 0 and tm % LANES == 0
    n_tiles = n_exp // te
    half = te // 2
    side_spec = lambda a: pl.BlockSpec((tm // LANES,) + a.shape[1:], lambda i, j: (i, 0, 0, 0, 0, 0))
    return pl.pallas_call(
        functools.partial(_peer_kernel, heads=heads, nk=nk, tm=tm, te=te, n_tiles=n_tiles),
        grid=(t // tm, n_tiles + 1),
        in_specs=[pl.BlockSpec((d // 2, tm), lambda i, j: (0, i)),
                  pl.BlockSpec((tm, d), lambda i, j: (i, 0)),
                  pl.BlockSpec((te // 2, d), lambda i, j: (jnp.minimum(j, n_tiles - 1), 0)),
                  pl.BlockSpec((d // 2, te), lambda i, j: (0, jnp.maximum(j - 1, 0))),
                  side_spec(np_side), side_spec(rp_side)],
        out_specs=pl.BlockSpec((tm, d), lambda i, j: (i, 0)),
        out_shape=jax.ShapeDtypeStruct((t, d), F32),
        scratch_shapes=[pltpu.VMEM((d, tm), F32),
                        pltpu.VMEM((tm // LANES, half, LANES), F32),
                        pltpu.VMEM((tm // LANES, half, LANES), F32),
                        pltpu.VMEM((tm // LANES, half // 2, LANES), jnp.uint32),
                        pltpu.VMEM((tm // LANES, half // 2, LANES), jnp.uint32)],
        compiler_params=pltpu.CompilerParams(
            dimension_semantics=("arbitrary", "arbitrary"), vmem_limit_bytes=VMEM_LIMIT),
        name="peer",
    )(xn, h2d, u_bf, vt_bf, np_side, rp_side)


def _ple_kernel(h_ref, p_ref, g_ref, wpg_ref, wpe_ref, gfin_ref, o_ref, *, final):
    h = h_ref[...]
    gate = jax.nn.sigmoid(_dot(_rmsnorm(h, g_ref[...]).astype(BF16), wpg_ref[...]))
    out = h + _dot(p_ref[...].astype(BF16), wpe_ref[...]) * gate
    if final:
        out = _rmsnorm(out, gfin_ref[...])
    o_ref[...] = out


def _ple(h2d, p2d, g_ple, w_pg, w_pe, g_final, *, tm, final):
    t, d = h2d.shape
    dp = p2d.shape[1]
    return pl.pallas_call(
        functools.partial(_ple_kernel, final=final),
        grid=(t // tm,),
        in_specs=[pl.BlockSpec((tm, d), lambda i: (i, 0)),
                  pl.BlockSpec((tm, dp), lambda i: (i, 0)),
                  pl.BlockSpec((1, d), lambda i: (0, 0)),
                  pl.BlockSpec((d, d), lambda i: (0, 0)),
                  pl.BlockSpec((dp, d), lambda i: (0, 0)),
                  pl.BlockSpec((1, d), lambda i: (0, 0))],
        out_specs=pl.BlockSpec((tm, d), lambda i: (i, 0)),
        out_shape=jax.ShapeDtypeStruct((t, d), F32),
        compiler_params=pltpu.CompilerParams(
            dimension_semantics=("arbitrary",), vmem_limit_bytes=VMEM_LIMIT),
        name="ple_final" if final else "ple",
    )(h2d, p2d, g_ple.reshape(1, d), w_pg, w_pe, g_final.reshape(1, d))


def kernel(x, p, g_mix, w_in, conv_a_w, conv_a_b, ln_a_g, ln_a_b, conv_b_w, ln_c_g, ln_c_b, w_s, b_s, w_out, g_ffn, w_q, sub_keys, expert_u, expert_v, g_ple, w_pe, w_pg, g_final):
    depth = w_in.shape[0]
    bsz, seq, d = x.shape
    t = bsz * seq
    d_c = ln_c_g.shape[-1]
    heads_c = d_c // HEAD_DIM

    lane = jnp.arange(d_c)
    gavg = jnp.where(lane[:, None] // HEAD_DIM == lane[None, :] // HEAD_DIM, 1.0 / HEAD_DIM, 0.0).astype(BF16)
    pos = jnp.arange(SG_BLOCK)
    causal = (pos[None, :] // CHUNK) <= (pos[:, None] // CHUNK)

    h = x
    for i in range(depth):
        wss = jnp.where(causal[None], w_s[i], 0.0).astype(BF16).reshape(heads_c * SG_BLOCK, SG_BLOCK)
        bsx = jnp.repeat(b_s[i].T, HEAD_DIM, axis=1)
        h = _mixer(h, g_mix[i], w_in[i].astype(BF16), conv_a_w[i], conv_a_b[i], ln_a_g[i], ln_a_b[i],
                   conv_b_w[i], ln_c_g[i], ln_c_b[i], wss, bsx, gavg, w_out[i].astype(BF16), ts=512)
        h2d = h.reshape(t, d)
        n_heads, _, n_keys, dk = sub_keys[i].shape
        keys = _pack_rows(jnp.swapaxes(sub_keys[i], 0, 1).reshape(2 * n_heads * n_keys, dk))
        keys = keys.reshape(2, n_heads, n_keys // 2, dk)
        xn, np_side, rp_side = _route(h2d, g_ffn[i], _pack_rows(w_q[i], transpose=True),
                                       keys[0], keys[1], tm=256)
        h2d = _peer(xn, h2d, _pack_rows(expert_u[i]), _pack_rows(expert_v[i], transpose=True),
                    np_side, rp_side, tm=512, te=1024)
        h2d = _ple(h2d, p[i].reshape(t, -1), g_ple[i], w_pg[i].astype(BF16), w_pe[i].astype(BF16),
                   g_final, tm=512, final=(i == depth - 1))
        h = h2d.reshape(bsz, seq, d)
    return h
```

```python
import functools

import jax
import jax.numpy as jnp
from jax import lax
from jax.experimental import pallas as pl
from jax.experimental.pallas import tpu as pltpu

EPS = 1e-6
HEAD_DIM = 64
CHUNK = 64
SG_BLOCK = 128
TOPK = 16
LANES = 128
SUBLANES = 8
HALO_A = 32
HALO_B = 8
VMEM_LIMIT = 56 * 1024 * 1024

F32 = jnp.float32
BF16 = jnp.bfloat16


def _dot(a, b):
    return jnp.dot(a, b, preferred_element_type=F32)


def _dot_nt(a, b):
    return lax.dot_general(a, b, (((1,), (1,)), ((), ())), preferred_element_type=F32)


def _pack_kernel(w_ref, o_ref, *, transpose):
    w = w_ref[...]
    if transpose:
        w = w.T
    o_ref[...] = pltpu.bitcast(w.astype(BF16), jnp.uint32)


def _pack_rows(w, *, transpose=False, rows=512):
    n, d = w.shape
    rows = min(rows, n)
    assert n % rows == 0 and rows % (2 * SUBLANES) == 0
    if transpose:
        in_spec = pl.BlockSpec((rows, d), lambda i: (i, 0))
        out_spec = pl.BlockSpec((d // 2, rows), lambda i: (0, i))
        out_shape = jax.ShapeDtypeStruct((d // 2, n), jnp.uint32)
    else:
        in_spec = pl.BlockSpec((rows, d), lambda i: (i, 0))
        out_spec = pl.BlockSpec((rows // 2, d), lambda i: (i, 0))
        out_shape = jax.ShapeDtypeStruct((n // 2, d), jnp.uint32)
    return pl.pallas_call(
        functools.partial(_pack_kernel, transpose=transpose),
        grid=(n // rows,), in_specs=[in_spec], out_specs=out_spec, out_shape=out_shape,
        compiler_params=pltpu.CompilerParams(
            dimension_semantics=("arbitrary",), vmem_limit_bytes=VMEM_LIMIT),
        name="pack_t" if transpose else "pack",
    )(w)


def _rmsnorm(x, g):
    return x * lax.rsqrt(jnp.mean(x * x, axis=-1, keepdims=True) + EPS) * g


def _dot_hilo(x, w):
    hi = x.astype(BF16)
    lo = (x - hi.astype(F32)).astype(BF16)
    return _dot(hi, w) + _dot(lo, w)


def _group_layernorm(x, gavg, g, b):
    mean = _dot_hilo(x, gavg)
    d = x - mean
    var = _dot_hilo(d * d, gavg)
    return d * lax.rsqrt(var + EPS) * g + b


def _gelu_tanh(x):
    c = 0.7978845608028654
    h = 0.5 * x
    return h + h * jnp.tanh(x * (c + (c * 0.044715) * (x * x)))


def _mixer_kernel(h_ref, gmix_ref, win_ref, caw_ref, cab_ref, lag_ref, lab_ref, cbw_ref,
                  lcg_ref, lcb_ref, wss_ref, bsx_ref, gavg_ref, wout_ref, o_ref,
                  ybuf, cbuf, ya_buf, ysh, *, d_a, d_b, d_c, ka, kb, ts, rows):
    @pl.when(pl.program_id(1) == 0)
    def _():
        ybuf[0:HALO_A, :] = jnp.zeros((HALO_A, d_a), F32)
        cbuf[0:HALO_B, :] = jnp.zeros((HALO_B, d_b), F32)

    h = h_ref[0]
    a = _rmsnorm(h, gmix_ref[...]).astype(BF16)
    z = _dot(a, win_ref[...])
    o_b = 2 * d_a
    o_c = o_b + 3 * d_b

    ybuf[HALO_A:HALO_A + ts, :] = z[:, :d_a] * jax.nn.sigmoid(z[:, d_a:2 * d_a])
    n_sh = ysh.shape[1]
    for s in range(1, SUBLANES):
        ysh[s - 1] = ybuf[s:s + n_sh, :]
    for r0 in range(0, ts, rows):
        acc = jnp.zeros((rows, d_a), F32) + cab_ref[...]
        for k in range(ka):
            off = HALO_A - (ka - 1) + k + r0
            s = off % SUBLANES
            win = ybuf[off:off + rows, :] if s == 0 else ysh[s - 1, off - s:off - s + rows, :]
            acc = acc + win * caw_ref[k:k + 1, :]
        ya_buf[r0:r0 + rows, :] = acc
    ybuf[0:HALO_A, :] = ybuf[ts:ts + HALO_A, :]
    ya = _group_layernorm(ya_buf[...], gavg_ref[...], lag_ref[...], lab_ref[...])
    ya = ya * jax.nn.sigmoid(ya)

    cbuf[HALO_B:HALO_B + ts, :] = z[:, o_b + d_b:o_b + 2 * d_b] * z[:, o_b + 2 * d_b:o_c]
    cb = jnp.zeros((ts, d_b), F32)
    for k in range(kb):
        off = HALO_B - (kb - 1) + k
        cb = cb + cbuf[off:off + ts, :] * cbw_ref[k:k + 1, :]
    cbuf[0:HALO_B, :] = cbuf[ts:ts + HALO_B, :]
    yb = z[:, o_b:o_b + d_b] * cb

    u = z[:, o_c:o_c + d_c]
    vln = _group_layernorm(z[:, o_c + d_c:], gavg_ref[...], lcg_ref[...], lcb_ref[...]).astype(BF16)
    heads_c = d_c // HEAD_DIM
    lane_head = lax.broadcasted_iota(jnp.int32, (SG_BLOCK, d_c), 1) // HEAD_DIM
    yc_blocks = []
    for n in range(ts // SG_BLOCK):
        sl = slice(n * SG_BLOCK, (n + 1) * SG_BLOCK)
        r = _dot(wss_ref[...], vln[sl, :])
        mixed = bsx_ref[...]
        for hd in range(heads_c):
            mixed = mixed + jnp.where(lane_head == hd, r[hd * SG_BLOCK:(hd + 1) * SG_BLOCK, :], 0.0)
        yc_blocks.append(u[sl, :] * mixed)
    yc = jnp.concatenate(yc_blocks, axis=0)

    ycat = jnp.concatenate([ya, yb, yc], axis=1).astype(BF16)
    o_ref[0] = h + _dot(ycat, wout_ref[...])


def _mixer(h, g_mix, w_in, conv_a_w, conv_a_b, ln_a_g, ln_a_b, conv_b_w, ln_c_g, ln_c_b,
           wss, bsx, gavg, w_out, *, ts):
    bsz, seq, d = h.shape
    ka, d_a = conv_a_w.shape
    kb, d_b = conv_b_w.shape
    d_c = ln_c_g.shape[-1]
    assert d_a == d_c and seq % ts == 0 and ts % SG_BLOCK == 0
    row = lambda v: v.reshape(1, -1)
    full = lambda arr: pl.BlockSpec(arr.shape, lambda b, s: (0,) * arr.ndim)
    operands = (row(g_mix), w_in, conv_a_w, row(conv_a_b), row(ln_a_g), row(ln_a_b), conv_b_w,
                row(ln_c_g), row(ln_c_b), wss, bsx, gavg, w_out)
    blk = pl.BlockSpec((1, ts, d), lambda b, s: (b, s, 0))
    return pl.pallas_call(
        functools.partial(_mixer_kernel, d_a=d_a, d_b=d_b, d_c=d_c, ka=ka, kb=kb, ts=ts, rows=64),
        grid=(bsz, seq // ts),
        in_specs=[blk] + [full(o) for o in operands],
        out_specs=blk,
        out_shape=jax.ShapeDtypeStruct(h.shape, F32),
        scratch_shapes=[pltpu.VMEM((ts + HALO_A, d_a), F32),
                        pltpu.VMEM((ts + HALO_B, d_b), F32),
                        pltpu.VMEM((ts, d_a), F32),
                        pltpu.VMEM((SUBLANES - 1, ts + HALO_A - SUBLANES, d_a), F32)],
        compiler_params=pltpu.CompilerParams(
            dimension_semantics=("arbitrary", "arbitrary"), vmem_limit_bytes=VMEM_LIMIT),
        name="mixer",
    )(h, *operands)


def _merge_exchange_pairs(n):
    pairs = []
    t = (n - 1).bit_length()
    p = 1 << (t - 1)
    while p > 0:
        q, r, d = 1 << (t - 1), 0, p
        while True:
            pairs.extend((i, i + d) for i in range(n - d) if (i & p) == r)
            if q == p:
                break
            d, q, r = q - p, q >> 1, p
        p >>= 1
    return pairs


def _top_values(s, k):
    n = s.shape[0] // SUBLANES
    lists = [s[i * SUBLANES:(i + 1) * SUBLANES, :] for i in range(n)]
    for i, j in _merge_exchange_pairs(n):
        lists[i], lists[j] = jnp.maximum(lists[i], lists[j]), jnp.minimum(lists[i], lists[j])
    vals = []
    for r in range(k):
        m = jnp.max(lists[0], axis=0, keepdims=True)
        vals.append(m)
        if r + 1 < k:
            head = lists[0] == m
            for i in range(min(n - 1, k - 1 - r)):
                lists[i] = jnp.where(head, lists[i + 1], lists[i])
            if n - 1 < k - 1 - r:
                lists[n - 1] = jnp.where(head, -jnp.inf, lists[n - 1])
    return jnp.concatenate(vals, axis=0)


def _pair_bf16(x):
    b = pltpu.bitcast(x.astype(BF16).astype(F32), jnp.uint32)
    return b | (b >> 16)


def _route_kernel(h_ref, g_ref, wqt_ref, k1_ref, k2_ref,
                  xn_ref, np_ref, rp_ref, q_scr, *, heads, dk, tm):
    xnt = _rmsnorm(h_ref[...], g_ref[...]).T.astype(BF16)
    xn_ref[...] = pltpu.bitcast(xnt, jnp.uint32)
    q_scr[...] = _dot(pltpu.bitcast(wqt_ref[...], BF16), xnt)

    def head_body(hd, carry):
        base = pl.multiple_of(hd * (2 * dk), 2 * dk)
        q1 = q_scr[pl.ds(base, dk), :].astype(BF16)
        q2 = q_scr[pl.ds(base + dk, dk), :].astype(BF16)
        s1 = _dot(pltpu.bitcast(k1_ref[hd], BF16), q1)
        s2 = _dot(pltpu.bitcast(k2_ref[hd], BF16), q2)
        for c in range(tm // LANES):
            cs = slice(c * LANES, (c + 1) * LANES)
            a1 = s1[:, cs]
            a2 = s2[:, cs]
            v1 = _top_values(a1, TOPK)
            v2 = _top_values(a2, TOPK)
            rho2 = jnp.full(a2.shape, float(TOPK), F32)
            for r in reversed(range(TOPK)):
                rho2 = jnp.where(a2 >= v2[r:r + 1, :], float(r), rho2)
            hs = TOPK // 2
            cand = jnp.concatenate(
                [v1[0:1, :] + v2]
                + [v1[a:a + 1, :] + v2[0:hs, :] for a in range(1, hs)]
                + [v1[hs:, :] + v2[0:1, :]], axis=0)
            tv = _top_values(cand, TOPK)
            tau = tv[TOPK - 1:TOPK, :]
            cmax = tv[0:1, :]
            z = jnp.sum(jnp.where(cand >= tau, jnp.exp(cand - cmax), 0.0), axis=0, keepdims=True)
            top_hi = jnp.sum(jnp.where(v1[0:1, :] + v2[hs:, :] >= tau, 1.0, 0.0), axis=0, keepdims=True)
            nrow = jnp.where(a1 == v1[0:1, :], top_hi, 0.0)
            for b in range(hs):
                nrow = nrow + jnp.where(a1 + v2[b:b + 1, :] >= tau, 1.0, 0.0)
            rows = (_pair_bf16(nrow), _pair_bf16(jnp.exp(a1 - v1[0:1, :]) / z))
            tiles = (pltpu.bitcast(rho2.astype(BF16), jnp.uint32),
                     pltpu.bitcast(jnp.exp(a2 - v2[0:1, :]).astype(BF16), jnp.uint32))
            for which in range(2):
                for g in range(rows[which].shape[0] // SUBLANES):
                    np_ref[c, g, hd, which] = rows[which][g * SUBLANES:(g + 1) * SUBLANES, :]
                for m in range(tiles[which].shape[0] // SUBLANES):
                    rp_ref[c, m, hd, which] = tiles[which][m * SUBLANES:(m + 1) * SUBLANES, :]
        return carry

    lax.fori_loop(0, heads, head_body, 0)


def _route(h2d, g_ffn, wqt, k1, k2, *, tm):
    t, d = h2d.shape
    heads, nk, dk = k1.shape[0], 2 * k1.shape[1], k1.shape[2]
    assert t % tm == 0 and tm % LANES == 0
    np_dims = (nk // SUBLANES, heads, 2, SUBLANES, LANES)
    rp_dims = (nk // (2 * SUBLANES), heads, 2, SUBLANES, LANES)
    side_spec = lambda dims: pl.BlockSpec((tm // LANES,) + dims, lambda i: (i, 0, 0, 0, 0, 0))
    side_shape = lambda dims: jax.ShapeDtypeStruct((t // LANES,) + dims, jnp.uint32)
    return pl.pallas_call(
        functools.partial(_route_kernel, heads=heads, dk=dk, tm=tm),
        grid=(t // tm,),
        in_specs=[pl.BlockSpec((tm, d), lambda i: (i, 0)),
                  pl.BlockSpec((1, d), lambda i: (0, 0)),
                  pl.BlockSpec(wqt.shape, lambda i: (0, 0)),
                  pl.BlockSpec(k1.shape, lambda i: (0, 0, 0)),
                  pl.BlockSpec(k2.shape, lambda i: (0, 0, 0))],
        out_specs=[pl.BlockSpec((d // 2, tm), lambda i: (0, i)), side_spec(np_dims), side_spec(rp_dims)],
        out_shape=[jax.ShapeDtypeStruct((d // 2, t), jnp.uint32), side_shape(np_dims), side_shape(rp_dims)],
        scratch_shapes=[pltpu.VMEM((2 * wqt.shape[0], tm), F32)],
        compiler_params=pltpu.CompilerParams(
            dimension_semantics=("arbitrary",), vmem_limit_bytes=VMEM_LIMIT),
        name="route",
    )(h2d, g_ffn.reshape(1, d), wqt, k1, k2)


def _peer_kernel(xn_ref, h_ref, u_ref, vt_ref, np_ref, rp_ref, p_ref, gple_ref, wpg_ref, wpe_ref,
                 gfin_ref, o_ref, acc, s_a, s_b, act_a, act_b, *, heads, nk, tm, te, n_tiles, final):
    j = pl.program_id(1)
    half = te // 2
    rows_half = half // nk
    pack = 2 * SUBLANES

    def scores(hf, s_buf):
        u_half = pltpu.bitcast(u_ref[hf * half // 2:(hf + 1) * half // 2, :], BF16)
        s = _dot(u_half, pltpu.bitcast(xn_ref[...], BF16))
        for c in range(tm // LANES):
            s_buf[c] = s[:, c * LANES:(c + 1) * LANES]

    def accumulate(act_buf, hf):
        vt_half = pltpu.bitcast(vt_ref[:, hf * half:(hf + 1) * half], BF16)
        act = jnp.concatenate([pltpu.bitcast(act_buf[c], BF16) for c in range(tm // LANES)], axis=1)
        acc[...] += _dot(vt_half, act)

    def gate(s_buf, act_buf, tile, hf):
        for c in range(tm // LANES):
            nr8 = [np_ref[c, tile, hd, 0] for hd in range(heads)]
            pr8 = [np_ref[c, tile, hd, 1] for hd in range(heads)]
            for k in range(rows_half):
                kk = hf * rows_half + k
                nr = [pltpu.bitcast(jnp.broadcast_to(nr8[hd][kk:kk + 1, :], (SUBLANES, LANES)), BF16)
                      for hd in range(heads)]
                pr = [pltpu.bitcast(jnp.broadcast_to(pr8[hd][kk:kk + 1, :], (SUBLANES, LANES)), BF16)
                      for hd in range(heads)]
                for m in range(nk // pack):
                    w = None
                    for hd in range(heads):
                        keep = pltpu.bitcast(rp_ref[c, m, hd, 0], BF16) < nr[hd]
                        p2 = pltpu.bitcast(rp_ref[c, m, hd, 1], BF16)
                        term = jnp.where(keep, p2, jnp.zeros((), BF16)) * pr[hd]
                        w = term if w is None else w + term
                    rs = slice(k * nk + m * pack, k * nk + (m + 1) * pack)
                    act = _gelu_tanh(s_buf[c, rs, :].astype(BF16)) * w
                    ws = slice((k * nk + m * pack) // 2, (k * nk + (m + 1) * pack) // 2)
                    act_buf[c, ws, :] = pltpu.bitcast(act, jnp.uint32)

    first = j == 0
    steady, steady_b = jnp.logical_and(j > 0, j < n_tiles), jnp.logical_and(j >= 1, j <= n_tiles - 1)
    last = j == n_tiles

    @pl.when(first)
    def _():
        acc[...] = jnp.zeros(acc.shape, F32)
        scores(0, s_a)

    @pl.when(first)
    def _():
        scores(1, s_b)
        gate(s_a, act_a, j, 0)

    @pl.when(steady)
    def _():
        gate(s_b, act_b, j - 1, 1)
        scores(0, s_a)
        accumulate(act_a, 0)

    @pl.when(steady_b)
    def _():
        gate(s_a, act_a, j, 0)
        scores(1, s_b)
        accumulate(act_b, 1)

    @pl.when(last)
    def _():
        gate(s_b, act_b, j - 1, 1)
        accumulate(act_a, 0)

    @pl.when(last)
    def _():
        accumulate(act_b, 1)
        out = h_ref[...] + acc[...].T
        ple_gate = jax.nn.sigmoid(_dot(_rmsnorm(out, gple_ref[...]).astype(BF16),
                                       pltpu.bitcast(wpg_ref[...], BF16)))
        out = out + _dot(p_ref[...].astype(BF16), pltpu.bitcast(wpe_ref[...], BF16)) * ple_gate
        if final:
            out = _rmsnorm(out, gfin_ref[...])
        o_ref[...] = out


def _peer(xn, h2d, u_bf, vt_bf, np_side, rp_side, p2d, g_ple, wpg, wpe, g_final, *, tm, te, final):
    t, d = h2d.shape
    n_exp = 2 * u_bf.shape[0]
    nk, heads = np_side.shape[1] * SUBLANES, np_side.shape[2]
    assert te == nk * SUBLANES and n_exp == nk * nk and t % tm == 0 and tm % LANES == 0
    n_tiles = n_exp // te
    half = te // 2
    side_spec = lambda a: pl.BlockSpec((tm // LANES,) + a.shape[1:], lambda i, j: (i, 0, 0, 0, 0, 0))
    return pl.pallas_call(
        functools.partial(_peer_kernel, heads=heads, nk=nk, tm=tm, te=te, n_tiles=n_tiles, final=final),
        grid=(t // tm, n_tiles + 1),
        in_specs=[pl.BlockSpec((d // 2, tm), lambda i, j: (0, i)),
                  pl.BlockSpec((tm, d), lambda i, j: (i, 0)),
                  pl.BlockSpec((te // 2, d), lambda i, j: (jnp.minimum(j, n_tiles - 1), 0)),
                  pl.BlockSpec((d // 2, te), lambda i, j: (0, jnp.maximum(j - 1, 0))),
                  side_spec(np_side), side_spec(rp_side),
                  pl.BlockSpec((tm, p2d.shape[1]), lambda i, j: (i, 0)),
                  pl.BlockSpec((1, d), lambda i, j: (0, 0)),
                  pl.BlockSpec(wpg.shape, lambda i, j: (0, 0)),
                  pl.BlockSpec(wpe.shape, lambda i, j: (0, 0)),
                  pl.BlockSpec((1, d), lambda i, j: (0, 0))],
        out_specs=pl.BlockSpec((tm, d), lambda i, j: (i, 0)),
        out_shape=jax.ShapeDtypeStruct((t, d), F32),
        scratch_shapes=[pltpu.VMEM((d, tm), F32),
                        pltpu.VMEM((tm // LANES, half, LANES), F32),
                        pltpu.VMEM((tm // LANES, half, LANES), F32),
                        pltpu.VMEM((tm // LANES, half // 2, LANES), jnp.uint32),
                        pltpu.VMEM((tm // LANES, half // 2, LANES), jnp.uint32)],
        compiler_params=pltpu.CompilerParams(
            dimension_semantics=("arbitrary", "arbitrary"), vmem_limit_bytes=VMEM_LIMIT),
        name="peer_final" if final else "peer",
    )(xn, h2d, u_bf, vt_bf, np_side, rp_side, p2d, g_ple.reshape(1, d), wpg, wpe, g_final.reshape(1, d))


def kernel(x, p, g_mix, w_in, conv_a_w, conv_a_b, ln_a_g, ln_a_b, conv_b_w, ln_c_g, ln_c_b, w_s, b_s, w_out, g_ffn, w_q, sub_keys, expert_u, expert_v, g_ple, w_pe, w_pg, g_final):
    depth = w_in.shape[0]
    bsz, seq, d = x.shape
    t = bsz * seq
    d_c = ln_c_g.shape[-1]
    heads_c = d_c // HEAD_DIM

    lane = jnp.arange(d_c)
    gavg = jnp.where(lane[:, None] // HEAD_DIM == lane[None, :] // HEAD_DIM, 1.0 / HEAD_DIM, 0.0).astype(BF16)
    pos = jnp.arange(SG_BLOCK)
    causal = (pos[None, :] // CHUNK) <= (pos[:, None] // CHUNK)

    h = x
    for i in range(depth):
        wss = jnp.where(causal[None], w_s[i], 0.0).astype(BF16).reshape(heads_c * SG_BLOCK, SG_BLOCK)
        bsx = jnp.repeat(b_s[i].T, HEAD_DIM, axis=1)
        h = _mixer(h, g_mix[i], w_in[i].astype(BF16), conv_a_w[i], conv_a_b[i], ln_a_g[i], ln_a_b[i],
                   conv_b_w[i], ln_c_g[i], ln_c_b[i], wss, bsx, gavg, w_out[i].astype(BF16), ts=512)
        h2d = h.reshape(t, d)
        n_heads, _, n_keys, dk = sub_keys[i].shape
        keys = _pack_rows(jnp.swapaxes(sub_keys[i], 0, 1).reshape(2 * n_heads * n_keys, dk))
        keys = keys.reshape(2, n_heads, n_keys // 2, dk)
        xn, np_side, rp_side = _route(h2d, g_ffn[i], _pack_rows(w_q[i], transpose=True),
                                       keys[0], keys[1], tm=256)
        h2d = _peer(xn, h2d, _pack_rows(expert_u[i]), _pack_rows(expert_v[i], transpose=True),
                    np_side, rp_side, p[i].reshape(t, -1), g_ple[i], _pack_rows(w_pg[i]), _pack_rows(w_pe[i]),
                    g_final, tm=512, te=1024, final=(i == depth - 1))
        h = h2d.reshape(bsz, seq, d)
    return h
```

```python
import functools

import jax
import jax.numpy as jnp
from jax import lax
from jax.experimental import pallas as pl
from jax.experimental.pallas import tpu as pltpu

EPS = 1e-6
HEAD_DIM = 64
CHUNK = 64
SG_BLOCK = 128
TOPK = 16
LANES = 128
SUBLANES = 8
HALO_A = 32
HALO_B = 8
VMEM_LIMIT = 56 * 1024 * 1024

F32 = jnp.float32
BF16 = jnp.bfloat16


def _dot(a, b):
    return jnp.dot(a, b, preferred_element_type=F32)


def _dot_nt(a, b):
    return lax.dot_general(a, b, (((1,), (1,)), ((), ())), preferred_element_type=F32)


def _pack_kernel(w_ref, o_ref, *, transpose):
    w = w_ref[...]
    if transpose:
        w = w.T
    o_ref[...] = pltpu.bitcast(w.astype(BF16), jnp.uint32)


def _pack_rows(w, *, transpose=False, rows=512):
    n, d = w.shape
    rows = min(rows, n)
    assert n % rows == 0 and rows % (2 * SUBLANES) == 0
    if transpose:
        in_spec = pl.BlockSpec((rows, d), lambda i: (i, 0))
        out_spec = pl.BlockSpec((d // 2, rows), lambda i: (0, i))
        out_shape = jax.ShapeDtypeStruct((d // 2, n), jnp.uint32)
    else:
        in_spec = pl.BlockSpec((rows, d), lambda i: (i, 0))
        out_spec = pl.BlockSpec((rows // 2, d), lambda i: (i, 0))
        out_shape = jax.ShapeDtypeStruct((n // 2, d), jnp.uint32)
    return pl.pallas_call(
        functools.partial(_pack_kernel, transpose=transpose),
        grid=(n // rows,), in_specs=[in_spec], out_specs=out_spec, out_shape=out_shape,
        compiler_params=pltpu.CompilerParams(
            dimension_semantics=("arbitrary",), vmem_limit_bytes=VMEM_LIMIT),
        name="pack_t" if transpose else "pack",
    )(w)


def _rmsnorm(x, g):
    return x * lax.rsqrt(jnp.mean(x * x, axis=-1, keepdims=True) + EPS) * g


def _dot_hilo(x, w):
    hi = x.astype(BF16)
    lo = (x - hi.astype(F32)).astype(BF16)
    return _dot(hi, w) + _dot(lo, w)


def _group_layernorm(x, gavg, g, b):
    mean = _dot_hilo(x, gavg)
    d = x - mean
    var = _dot_hilo(d * d, gavg)
    return d * lax.rsqrt(var + EPS) * g + b


def _gelu_tanh(x):
    c = 0.7978845608028654
    h = 0.5 * x
    return h + h * jnp.tanh(x * (c + (c * 0.044715) * (x * x)))


def _mixer_kernel(h_ref, gmix_ref, win_ref, caw_ref, cab_ref, lag_ref, lab_ref, cbw_ref,
                  lcg_ref, lcb_ref, wss_ref, bsx_ref, gavg_ref, wout_ref, o_ref,
                  ybuf, cbuf, ya_buf, ysh, *, d_a, d_b, d_c, ka, kb, ts, rows):
    @pl.when(pl.program_id(1) == 0)
    def _():
        ybuf[0:HALO_A, :] = jnp.zeros((HALO_A, d_a), F32)
        cbuf[0:HALO_B, :] = jnp.zeros((HALO_B, d_b), F32)

    h = h_ref[0]
    a = _rmsnorm(h, gmix_ref[...]).astype(BF16)
    z = _dot(a, win_ref[...])
    o_b = 2 * d_a
    o_c = o_b + 3 * d_b

    ybuf[HALO_A:HALO_A + ts, :] = z[:, :d_a] * jax.nn.sigmoid(z[:, d_a:2 * d_a])
    n_sh = ysh.shape[1]
    for s in range(1, SUBLANES):
        ysh[s - 1] = ybuf[s:s + n_sh, :]
    for r0 in range(0, ts, rows):
        acc = jnp.zeros((rows, d_a), F32) + cab_ref[...]
        for k in range(ka):
            off = HALO_A - (ka - 1) + k + r0
            s = off % SUBLANES
            win = ybuf[off:off + rows, :] if s == 0 else ysh[s - 1, off - s:off - s + rows, :]
            acc = acc + win * caw_ref[k:k + 1, :]
        ya_buf[r0:r0 + rows, :] = acc
    ybuf[0:HALO_A, :] = ybuf[ts:ts + HALO_A, :]
    ya = _group_layernorm(ya_buf[...], gavg_ref[...], lag_ref[...], lab_ref[...])
    ya = ya * jax.nn.sigmoid(ya)

    cbuf[HALO_B:HALO_B + ts, :] = z[:, o_b + d_b:o_b + 2 * d_b] * z[:, o_b + 2 * d_b:o_c]
    cb = jnp.zeros((ts, d_b), F32)
    for k in range(kb):
        off = HALO_B - (kb - 1) + k
        cb = cb + cbuf[off:off + ts, :] * cbw_ref[k:k + 1, :]
    cbuf[0:HALO_B, :] = cbuf[ts:ts + HALO_B, :]
    yb = z[:, o_b:o_b + d_b] * cb

    u = z[:, o_c:o_c + d_c]
    vln = _group_layernorm(z[:, o_c + d_c:], gavg_ref[...], lcg_ref[...], lcb_ref[...]).astype(BF16)
    heads_c = d_c // HEAD_DIM
    lane_head = lax.broadcasted_iota(jnp.int32, (SG_BLOCK, d_c), 1) // HEAD_DIM
    yc_blocks = []
    for n in range(ts // SG_BLOCK):
        sl = slice(n * SG_BLOCK, (n + 1) * SG_BLOCK)
        r = _dot(wss_ref[...], vln[sl, :])
        mixed = bsx_ref[...]
        for hd in range(heads_c):
            mixed = mixed + jnp.where(lane_head == hd, r[hd * SG_BLOCK:(hd + 1) * SG_BLOCK, :], 0.0)
        yc_blocks.append(u[sl, :] * mixed)
    yc = jnp.concatenate(yc_blocks, axis=0)

    ycat = jnp.concatenate([ya, yb, yc], axis=1).astype(BF16)
    o_ref[0] = h + _dot(ycat, wout_ref[...])


def _mixer(h, g_mix, w_in, conv_a_w, conv_a_b, ln_a_g, ln_a_b, conv_b_w, ln_c_g, ln_c_b,
           wss, bsx, gavg, w_out, *, ts):
    bsz, seq, d = h.shape
    ka, d_a = conv_a_w.shape
    kb, d_b = conv_b_w.shape
    d_c = ln_c_g.shape[-1]
    assert d_a == d_c and seq % ts == 0 and ts % SG_BLOCK == 0
    row = lambda v: v.reshape(1, -1)
    full = lambda arr: pl.BlockSpec(arr.shape, lambda b, s: (0,) * arr.ndim)
    operands = (row(g_mix), w_in, conv_a_w, row(conv_a_b), row(ln_a_g), row(ln_a_b), conv_b_w,
                row(ln_c_g), row(ln_c_b), wss, bsx, gavg, w_out)
    blk = pl.BlockSpec((1, ts, d), lambda b, s: (b, s, 0))
    return pl.pallas_call(
        functools.partial(_mixer_kernel, d_a=d_a, d_b=d_b, d_c=d_c, ka=ka, kb=kb, ts=ts, rows=64),
        grid=(bsz, seq // ts),
        in_specs=[blk] + [full(o) for o in operands],
        out_specs=blk,
        out_shape=jax.ShapeDtypeStruct(h.shape, F32),
        scratch_shapes=[pltpu.VMEM((ts + HALO_A, d_a), F32),
                        pltpu.VMEM((ts + HALO_B, d_b), F32),
                        pltpu.VMEM((ts, d_a), F32),
                        pltpu.VMEM((SUBLANES - 1, ts + HALO_A - SUBLANES, d_a), F32)],
        compiler_params=pltpu.CompilerParams(
            dimension_semantics=("arbitrary", "arbitrary"), vmem_limit_bytes=VMEM_LIMIT),
        name="mixer",
    )(h, *operands)


def _merge_exchange_pairs(n):
    pairs = []
    t = (n - 1).bit_length()
    p = 1 << (t - 1)
    while p > 0:
        q, r, d = 1 << (t - 1), 0, p
        while True:
            pairs.extend((i, i + d) for i in range(n - d) if (i & p) == r)
            if q == p:
                break
            d, q, r = q - p, q >> 1, p
        p >>= 1
    return pairs


def _top_values(s, k):
    n = s.shape[0] // SUBLANES
    lists = [s[i * SUBLANES:(i + 1) * SUBLANES, :] for i in range(n)]
    for i, j in _merge_exchange_pairs(n):
        lists[i], lists[j] = jnp.maximum(lists[i], lists[j]), jnp.minimum(lists[i], lists[j])
    vals = []
    for r in range(k):
        m = jnp.max(lists[0], axis=0, keepdims=True)
        vals.append(m)
        if r + 1 < k:
            head = lists[0] == m
            for i in range(min(n - 1, k - 1 - r)):
                lists[i] = jnp.where(head, lists[i + 1], lists[i])
            if n - 1 < k - 1 - r:
                lists[n - 1] = jnp.where(head, -jnp.inf, lists[n - 1])
    return jnp.concatenate(vals, axis=0)


def _pair_bf16(x):
    b = pltpu.bitcast(x.astype(BF16).astype(F32), jnp.uint32)
    return b | (b >> 16)


def _route_kernel(h_ref, g_ref, wqt_ref, k1_ref, k2_ref,
                  xn_ref, np_ref, rp_ref, q_scr, *, heads, dk, tm):
    xnt = _rmsnorm(h_ref[...], g_ref[...]).T.astype(BF16)
    xn_ref[...] = pltpu.bitcast(xnt, jnp.uint32)
    q_scr[...] = _dot(pltpu.bitcast(wqt_ref[...], BF16), xnt)

    def head_body(hd, carry):
        base = pl.multiple_of(hd * (2 * dk), 2 * dk)
        q1 = q_scr[pl.ds(base, dk), :].astype(BF16)
        q2 = q_scr[pl.ds(base + dk, dk), :].astype(BF16)
        s1 = _dot(pltpu.bitcast(k1_ref[hd], BF16), q1)
        s2 = _dot(pltpu.bitcast(k2_ref[hd], BF16), q2)
        for c in range(tm // LANES):
            cs = slice(c * LANES, (c + 1) * LANES)
            a1 = s1[:, cs]
            a2 = s2[:, cs]
            v1 = _top_values(a1, TOPK)
            v2 = _top_values(a2, TOPK)
            rho2 = jnp.full(a2.shape, float(TOPK), F32)
            for r in reversed(range(TOPK)):
                rho2 = jnp.where(a2 >= v2[r:r + 1, :], float(r), rho2)
            hs = TOPK // 2
            cand = jnp.concatenate(
                [v1[0:1, :] + v2]
                + [v1[a:a + 1, :] + v2[0:hs, :] for a in range(1, hs)]
                + [v1[hs:, :] + v2[0:1, :]], axis=0)
            tv = _top_values(cand, TOPK)
            tau = tv[TOPK - 1:TOPK, :]
            cmax = tv[0:1, :]
            z = jnp.sum(jnp.where(cand >= tau, jnp.exp(cand - cmax), 0.0), axis=0, keepdims=True)
            top_hi = jnp.sum(jnp.where(v1[0:1, :] + v2[hs:, :] >= tau, 1.0, 0.0), axis=0, keepdims=True)
            nrow = jnp.where(a1 == v1[0:1, :], top_hi, 0.0)
            for b in range(hs):
                nrow = nrow + jnp.where(a1 + v2[b:b + 1, :] >= tau, 1.0, 0.0)
            rows = (_pair_bf16(nrow), _pair_bf16(jnp.exp(a1 - v1[0:1, :]) / z))
            tiles = (pltpu.bitcast(rho2.astype(BF16), jnp.uint32),
                     pltpu.bitcast(jnp.exp(a2 - v2[0:1, :]).astype(BF16), jnp.uint32))
            for which in range(2):
                for g in range(rows[which].shape[0] // SUBLANES):
                    np_ref[c, g, hd, which] = rows[which][g * SUBLANES:(g + 1) * SUBLANES, :]
                for m in range(tiles[which].shape[0] // SUBLANES):
                    rp_ref[c, m, hd, which] = tiles[which][m * SUBLANES:(m + 1) * SUBLANES, :]
        return carry

    lax.fori_loop(0, heads, head_body, 0)


def _route(h2d, g_ffn, wqt, k1, k2, *, tm):
    t, d = h2d.shape
    heads, nk, dk = k1.shape[0], 2 * k1.shape[1], k1.shape[2]
    assert t % tm == 0 and tm % LANES == 0
    np_dims = (nk // SUBLANES, heads, 2, SUBLANES, LANES)
    rp_dims = (nk // (2 * SUBLANES), heads, 2, SUBLANES, LANES)
    side_spec = lambda dims: pl.BlockSpec((tm // LANES,) + dims, lambda i: (i, 0, 0, 0, 0, 0))
    side_shape = lambda dims: jax.ShapeDtypeStruct((t // LANES,) + dims, jnp.uint32)
    return pl.pallas_call(
        functools.partial(_route_kernel, heads=heads, dk=dk, tm=tm),
        grid=(t // tm,),
        in_specs=[pl.BlockSpec((tm, d), lambda i: (i, 0)),
                  pl.BlockSpec((1, d), lambda i: (0, 0)),
                  pl.BlockSpec(wqt.shape, lambda i: (0, 0)),
                  pl.BlockSpec(k1.shape, lambda i: (0, 0, 0)),
                  pl.BlockSpec(k2.shape, lambda i: (0, 0, 0))],
        out_specs=[pl.BlockSpec((d // 2, tm), lambda i: (0, i)), side_spec(np_dims), side_spec(rp_dims)],
        out_shape=[jax.ShapeDtypeStruct((d // 2, t), jnp.uint32), side_shape(np_dims), side_shape(rp_dims)],
        scratch_shapes=[pltpu.VMEM((2 * wqt.shape[0], tm), F32)],
        compiler_params=pltpu.CompilerParams(
            dimension_semantics=("arbitrary",), vmem_limit_bytes=VMEM_LIMIT),
        name="route",
    )(h2d, g_ffn.reshape(1, d), wqt, k1, k2)


def _peer_kernel(xn_ref, h_ref, u_ref, vt_ref, np_ref, rp_ref, p_ref, gple_ref, wpg_ref, wpe_ref,
                 gfin_ref, o_ref, acc, s_a, s_b, act_a, act_b, *, heads, nk, tm, te, n_tiles, final):
    j = pl.program_id(1)
    half = te // 2
    rows_half = half // nk
    pack = 2 * SUBLANES

    def scores(hf, s_buf):
        u_half = pltpu.bitcast(u_ref[hf * half // 2:(hf + 1) * half // 2, :], BF16)
        s = _dot(u_half, pltpu.bitcast(xn_ref[...], BF16))
        for c in range(tm // LANES):
            s_buf[c] = s[:, c * LANES:(c + 1) * LANES]

    def accumulate(act_buf, hf):
        vt_half = pltpu.bitcast(vt_ref[:, hf * half:(hf + 1) * half], BF16)
        act = jnp.concatenate([pltpu.bitcast(act_buf[c], BF16) for c in range(tm // LANES)], axis=1)
        acc[...] += _dot(vt_half, act)

    def gate(s_buf, act_buf, tile, hf):
        for c in range(tm // LANES):
            nr8 = [np_ref[c, tile, hd, 0] for hd in range(heads)]
            pr8 = [np_ref[c, tile, hd, 1] for hd in range(heads)]
            for k in range(rows_half):
                kk = hf * rows_half + k
                nr = [pltpu.bitcast(jnp.broadcast_to(nr8[hd][kk:kk + 1, :], (SUBLANES, LANES)), BF16)
                      for hd in range(heads)]
                pr = [pltpu.bitcast(jnp.broadcast_to(pr8[hd][kk:kk + 1, :], (SUBLANES, LANES)), BF16)
                      for hd in range(heads)]
                for m in range(nk // pack):
                    w = None
                    for hd in range(heads):
                        keep = pltpu.bitcast(rp_ref[c, m, hd, 0], BF16) < nr[hd]
                        p2 = pltpu.bitcast(rp_ref[c, m, hd, 1], BF16)
                        term = jnp.where(keep, p2, jnp.zeros((), BF16)) * pr[hd]
                        w = term if w is None else w + term
                    rs = slice(k * nk + m * pack, k * nk + (m + 1) * pack)
                    act = _gelu_tanh(s_buf[c, rs, :].astype(BF16)) * w
                    ws = slice((k * nk + m * pack) // 2, (k * nk + (m + 1) * pack) // 2)
                    act_buf[c, ws, :] = pltpu.bitcast(act, jnp.uint32)

    first = j == 0
    steady, steady_b = jnp.logical_and(j > 0, j < n_tiles), jnp.logical_and(j >= 1, j <= n_tiles - 1)
    last = j == n_tiles

    @pl.when(first)
    def _():
        acc[...] = jnp.zeros(acc.shape, F32)
        scores(0, s_a)

    @pl.when(first)
    def _():
        scores(1, s_b)
        gate(s_a, act_a, j, 0)

    @pl.when(steady)
    def _():
        gate(s_b, act_b, j - 1, 1)
        scores(0, s_a)
        accumulate(act_a, 0)

    @pl.when(steady_b)
    def _():
        gate(s_a, act_a, j, 0)
        scores(1, s_b)
        accumulate(act_b, 1)

    @pl.when(last)
    def _():
        gate(s_b, act_b, j - 1, 1)
        accumulate(act_a, 0)

    @pl.when(last)
    def _():
        accumulate(act_b, 1)
        out = h_ref[...] + acc[...].T
        ple_gate = jax.nn.sigmoid(_dot(_rmsnorm(out, gple_ref[...]).astype(BF16),
                                       pltpu.bitcast(wpg_ref[...], BF16)))
        out = out + _dot(p_ref[...].astype(BF16), pltpu.bitcast(wpe_ref[...], BF16)) * ple_gate
        if final:
            out = _rmsnorm(out, gfin_ref[...])
        o_ref[...] = out


def _peer(xn, h2d, u_bf, vt_bf, np_side, rp_side, p2d, g_ple, wpg, wpe, g_final, *, tm, te, final):
    t, d = h2d.shape
    n_exp = 2 * u_bf.shape[0]
    nk, heads = np_side.shape[1] * SUBLANES, np_side.shape[2]
    assert te == nk * SUBLANES and n_exp == nk * nk and t % tm == 0 and tm % LANES == 0
    n_tiles = n_exp // te
    half = te // 2
    side_spec = lambda a: pl.BlockSpec((tm // LANES,) + a.shape[1:], lambda i, j: (i, 0, 0, 0, 0, 0))
    return pl.pallas_call(
        functools.partial(_peer_kernel, heads=heads, nk=nk, tm=tm, te=te, n_tiles=n_tiles, final=final),
        grid=(t // tm, n_tiles + 1),
        in_specs=[pl.BlockSpec((d // 2, tm), lambda i, j: (0, i)),
                  pl.BlockSpec((tm, d), lambda i, j: (i, 0)),
                  pl.BlockSpec((te // 2, d), lambda i, j: (jnp.minimum(j, n_tiles - 1), 0)),
                  pl.BlockSpec((d // 2, te), lambda i, j: (0, jnp.maximum(j - 1, 0))),
                  side_spec(np_side), side_spec(rp_side),
                  pl.BlockSpec((tm, p2d.shape[1]), lambda i, j: (i, 0)),
                  pl.BlockSpec((1, d), lambda i, j: (0, 0)),
                  pl.BlockSpec(wpg.shape, lambda i, j: (0, 0)),
                  pl.BlockSpec(wpe.shape, lambda i, j: (0, 0)),
                  pl.BlockSpec((1, d), lambda i, j: (0, 0))],
        out_specs=pl.BlockSpec((tm, d), lambda i, j: (i, 0)),
        out_shape=jax.ShapeDtypeStruct((t, d), F32),
        scratch_shapes=[pltpu.VMEM((d, tm), F32),
                        pltpu.VMEM((tm // LANES, half, LANES), F32),
                        pltpu.VMEM((tm // LANES, half, LANES), F32),
                        pltpu.VMEM((tm // LANES, half // 2, LANES), jnp.uint32),
                        pltpu.VMEM((tm // LANES, half // 2, LANES), jnp.uint32)],
        compiler_params=pltpu.CompilerParams(
            dimension_semantics=("arbitrary", "arbitrary"), vmem_limit_bytes=VMEM_LIMIT),
        name="peer_final" if final else "peer",
    )(xn, h2d, u_bf, vt_bf, np_side, rp_side, p2d, g_ple.reshape(1, d), wpg, wpe, g_final.reshape(1, d))


def kernel(x, p, g_mix, w_in, conv_a_w, conv_a_b, ln_a_g, ln_a_b, conv_b_w, ln_c_g, ln_c_b, w_s, b_s, w_out, g_ffn, w_q, sub_keys, expert_u, expert_v, g_ple, w_pe, w_pg, g_final):
    depth = w_in.shape[0]
    bsz, seq, d = x.shape
    t = bsz * seq
    d_c = ln_c_g.shape[-1]
    heads_c = d_c // HEAD_DIM

    lane = jnp.arange(d_c)
    gavg = jnp.where(lane[:, None] // HEAD_DIM == lane[None, :] // HEAD_DIM, 1.0 / HEAD_DIM, 0.0).astype(BF16)
    pos = jnp.arange(SG_BLOCK)
    causal = (pos[None, :] // CHUNK) <= (pos[:, None] // CHUNK)

    h = x
    for i in range(depth):
        wss = jnp.where(causal[None], w_s[i], 0.0).astype(BF16).reshape(heads_c * SG_BLOCK, SG_BLOCK)
        bsx = jnp.repeat(b_s[i].T, HEAD_DIM, axis=1)
        h = _mixer(h, g_mix[i], w_in[i].astype(BF16), conv_a_w[i], conv_a_b[i], ln_a_g[i], ln_a_b[i],
                   conv_b_w[i], ln_c_g[i], ln_c_b[i], wss, bsx, gavg, w_out[i].astype(BF16), ts=512)
        h2d = h.reshape(t, d)
        n_heads, _, n_keys, dk = sub_keys[i].shape
        keys = _pack_rows(jnp.swapaxes(sub_keys[i], 0, 1).reshape(2 * n_heads * n_keys, dk))
        keys = keys.reshape(2, n_heads, n_keys // 2, dk)
        xn, np_side, rp_side = _route(h2d, g_ffn[i], _pack_rows(w_q[i], transpose=True),
                                       keys[0], keys[1], tm=512)
        h2d = _peer(xn, h2d, _pack_rows(expert_u[i]), _pack_rows(expert_v[i], transpose=True),
                    np_side, rp_side, p[i].reshape(t, -1), g_ple[i], _pack_rows(w_pg[i]), _pack_rows(w_pe[i]),
                    g_final, tm=512, te=1024, final=(i == depth - 1))
        h = h2d.reshape(bsz, seq, d)
    return h
```

```python
import functools

import jax
import jax.numpy as jnp
from jax import lax
from jax.experimental import pallas as pl
from jax.experimental.pallas import tpu as pltpu

EPS = 1e-6
HEAD_DIM = 64
CHUNK = 64
SG_BLOCK = 128
TOPK = 16
LANES = 128
SUBLANES = 8
HALO_A = 32
HALO_B = 8
VMEM_LIMIT = 56 * 1024 * 1024

F32 = jnp.float32
BF16 = jnp.bfloat16


def _dot(a, b):
    return jnp.dot(a, b, preferred_element_type=F32)


def _dot_nt(a, b):
    return lax.dot_general(a, b, (((1,), (1,)), ((), ())), preferred_element_type=F32)


def _pack_kernel(w_ref, o_ref, *, transpose):
    w = w_ref[...]
    if transpose:
        w = w.T
    o_ref[...] = pltpu.bitcast(w.astype(BF16), jnp.uint32)


def _pack_rows(w, *, layer=None, transpose=False, rows=512):
    n, d = w.shape[-2:]
    rows = min(rows, n)
    assert n % rows == 0 and rows % (2 * SUBLANES) == 0 and (w.ndim == 3) == (layer is not None)
    if layer is None:
        in_spec = pl.BlockSpec((rows, d), lambda i: (i, 0))
    else:
        in_spec = pl.BlockSpec((None, rows, d), lambda i: (layer, i, 0))
    if transpose:
        out_spec = pl.BlockSpec((d // 2, rows), lambda i: (0, i))
        out_shape = jax.ShapeDtypeStruct((d // 2, n), jnp.uint32)
    else:
        out_spec = pl.BlockSpec((rows // 2, d), lambda i: (i, 0))
        out_shape = jax.ShapeDtypeStruct((n // 2, d), jnp.uint32)
    return pl.pallas_call(
        functools.partial(_pack_kernel, transpose=transpose),
        grid=(n // rows,), in_specs=[in_spec], out_specs=out_spec, out_shape=out_shape,
        compiler_params=pltpu.CompilerParams(
            dimension_semantics=("arbitrary",), vmem_limit_bytes=VMEM_LIMIT),
        name="pack_t" if transpose else "pack",
    )(w)


def _rmsnorm(x, g):
    return x * lax.rsqrt(jnp.mean(x * x, axis=-1, keepdims=True) + EPS) * g


def _dot_hilo(x, w):
    hi = x.astype(BF16)
    lo = (x - hi.astype(F32)).astype(BF16)
    return _dot(hi, w) + _dot(lo, w)


def _group_layernorm(x, gavg, g, b):
    mean = _dot_hilo(x, gavg)
    d = x - mean
    var = _dot_hilo(d * d, gavg)
    return d * lax.rsqrt(var + EPS) * g + b


def _gelu_tanh(x):
    c = 0.7978845608028654
    h = 0.5 * x
    return h + h * jnp.tanh(x * (c + (c * 0.044715) * (x * x)))


def _mixer_kernel(h_ref, gmix_ref, win_ref, caw_ref, cab_ref, lag_ref, lab_ref, cbw_ref,
                  lcg_ref, lcb_ref, wss_ref, bsx_ref, gavg_ref, wout_ref, o_ref,
                  ybuf, cbuf, ya_buf, ysh, *, d_a, d_b, d_c, ka, kb, ts, rows):
    @pl.when(pl.program_id(1) == 0)
    def _():
        ybuf[0:HALO_A, :] = jnp.zeros((HALO_A, d_a), F32)
        cbuf[0:HALO_B, :] = jnp.zeros((HALO_B, d_b), F32)

    h = h_ref[0]
    a = _rmsnorm(h, gmix_ref[...]).astype(BF16)
    z = _dot(a, win_ref[...])
    o_b = 2 * d_a
    o_c = o_b + 3 * d_b

    ybuf[HALO_A:HALO_A + ts, :] = z[:, :d_a] * jax.nn.sigmoid(z[:, d_a:2 * d_a])
    n_sh = ysh.shape[1]
    for s in range(1, SUBLANES):
        ysh[s - 1] = ybuf[s:s + n_sh, :]
    for r0 in range(0, ts, rows):
        acc = jnp.zeros((rows, d_a), F32) + cab_ref[...]
        for k in range(ka):
            off = HALO_A - (ka - 1) + k + r0
            s = off % SUBLANES
            win = ybuf[off:off + rows, :] if s == 0 else ysh[s - 1, off - s:off - s + rows, :]
            acc = acc + win * caw_ref[k:k + 1, :]
        ya_buf[r0:r0 + rows, :] = acc
    ybuf[0:HALO_A, :] = ybuf[ts:ts + HALO_A, :]
    ya = _group_layernorm(ya_buf[...], gavg_ref[...], lag_ref[...], lab_ref[...])
    ya = ya * jax.nn.sigmoid(ya)

    cbuf[HALO_B:HALO_B + ts, :] = z[:, o_b + d_b:o_b + 2 * d_b] * z[:, o_b + 2 * d_b:o_c]
    cb = jnp.zeros((ts, d_b), F32)
    for k in range(kb):
        off = HALO_B - (kb - 1) + k
        cb = cb + cbuf[off:off + ts, :] * cbw_ref[k:k + 1, :]
    cbuf[0:HALO_B, :] = cbuf[ts:ts + HALO_B, :]
    yb = z[:, o_b:o_b + d_b] * cb

    u = z[:, o_c:o_c + d_c]
    vln = _group_layernorm(z[:, o_c + d_c:], gavg_ref[...], lcg_ref[...], lcb_ref[...]).astype(BF16)
    heads_c = d_c // HEAD_DIM
    lane_head = lax.broadcasted_iota(jnp.int32, (SG_BLOCK, d_c), 1) // HEAD_DIM
    yc_blocks = []
    for n in range(ts // SG_BLOCK):
        sl = slice(n * SG_BLOCK, (n + 1) * SG_BLOCK)
        r = _dot(wss_ref[...], vln[sl, :])
        mixed = bsx_ref[...]
        for hd in range(heads_c):
            mixed = mixed + jnp.where(lane_head == hd, r[hd * SG_BLOCK:(hd + 1) * SG_BLOCK, :], 0.0)
        yc_blocks.append(u[sl, :] * mixed)
    yc = jnp.concatenate(yc_blocks, axis=0)

    ycat = jnp.concatenate([ya, yb, yc], axis=1).astype(BF16)
    o_ref[0] = h + _dot(ycat, wout_ref[...])


def _mixer(h, g_mix, w_in, conv_a_w, conv_a_b, ln_a_g, ln_a_b, conv_b_w, ln_c_g, ln_c_b,
           wss, bsx, gavg, w_out, *, ts):
    bsz, seq, d = h.shape
    ka, d_a = conv_a_w.shape
    kb, d_b = conv_b_w.shape
    d_c = ln_c_g.shape[-1]
    assert d_a == d_c and seq % ts == 0 and ts % SG_BLOCK == 0
    row = lambda v: v.reshape(1, -1)
    full = lambda arr: pl.BlockSpec(arr.shape, lambda b, s: (0,) * arr.ndim)
    operands = (row(g_mix), w_in, conv_a_w, row(conv_a_b), row(ln_a_g), row(ln_a_b), conv_b_w,
                row(ln_c_g), row(ln_c_b), wss, bsx, gavg, w_out)
    blk = pl.BlockSpec((1, ts, d), lambda b, s: (b, s, 0))
    return pl.pallas_call(
        functools.partial(_mixer_kernel, d_a=d_a, d_b=d_b, d_c=d_c, ka=ka, kb=kb, ts=ts, rows=64),
        grid=(bsz, seq // ts),
        in_specs=[blk] + [full(o) for o in operands],
        out_specs=blk,
        out_shape=jax.ShapeDtypeStruct(h.shape, F32),
        scratch_shapes=[pltpu.VMEM((ts + HALO_A, d_a), F32),
                        pltpu.VMEM((ts + HALO_B, d_b), F32),
                        pltpu.VMEM((ts, d_a), F32),
                        pltpu.VMEM((SUBLANES - 1, ts + HALO_A - SUBLANES, d_a), F32)],
        compiler_params=pltpu.CompilerParams(
            dimension_semantics=("arbitrary", "arbitrary"), vmem_limit_bytes=VMEM_LIMIT),
        name="mixer",
    )(h, *operands)


def _merge_exchange_pairs(n):
    pairs = []
    t = (n - 1).bit_length()
    p = 1 << (t - 1)
    while p > 0:
        q, r, d = 1 << (t - 1), 0, p
        while True:
            pairs.extend((i, i + d) for i in range(n - d) if (i & p) == r)
            if q == p:
                break
            d, q, r = q - p, q >> 1, p
        p >>= 1
    return pairs


def _top_values(s, k):
    n = s.shape[0] // SUBLANES
    lists = [s[i * SUBLANES:(i + 1) * SUBLANES, :] for i in range(n)]
    for i, j in _merge_exchange_pairs(n):
        lists[i], lists[j] = jnp.maximum(lists[i], lists[j]), jnp.minimum(lists[i], lists[j])
    vals = []
    for r in range(k):
        m = jnp.max(lists[0], axis=0, keepdims=True)
        vals.append(m)
        if r + 1 < k:
            head = lists[0] == m
            for i in range(min(n - 1, k - 1 - r)):
                lists[i] = jnp.where(head, lists[i + 1], lists[i])
            if n - 1 < k - 1 - r:
                lists[n - 1] = jnp.where(head, -jnp.inf, lists[n - 1])
    return jnp.concatenate(vals, axis=0)


def _pair_bf16(x):
    b = pltpu.bitcast(x.astype(BF16).astype(F32), jnp.uint32)
    return b | (b >> 16)


def _route_kernel(h_ref, g_ref, wqt_ref, k1_ref, k2_ref,
                  xn_ref, np_ref, rp_ref, q_scr, *, heads, dk, tm):
    xnt = _rmsnorm(h_ref[...], g_ref[...]).T.astype(BF16)
    xn_ref[...] = pltpu.bitcast(xnt, jnp.uint32)
    q_scr[...] = _dot(pltpu.bitcast(wqt_ref[...], BF16), xnt)

    def head_body(hd, carry):
        base = pl.multiple_of(hd * (2 * dk), 2 * dk)
        q1 = q_scr[pl.ds(base, dk), :].astype(BF16)
        q2 = q_scr[pl.ds(base + dk, dk), :].astype(BF16)
        s1 = _dot(pltpu.bitcast(k1_ref[hd], BF16), q1)
        s2 = _dot(pltpu.bitcast(k2_ref[hd], BF16), q2)
        for c in range(tm // LANES):
            cs = slice(c * LANES, (c + 1) * LANES)
            a1 = s1[:, cs]
            a2 = s2[:, cs]
            v1 = _top_values(a1, TOPK)
            v2 = _top_values(a2, TOPK)
            rho2 = jnp.full(a2.shape, float(TOPK), F32)
            for r in reversed(range(TOPK)):
                rho2 = jnp.where(a2 >= v2[r:r + 1, :], float(r), rho2)
            hs = TOPK // 2
            cand = jnp.concatenate(
                [v1[0:1, :] + v2]
                + [v1[a:a + 1, :] + v2[0:hs, :] for a in range(1, hs)]
                + [v1[hs:, :] + v2[0:1, :]], axis=0)
            tv = _top_values(cand, TOPK)
            tau = tv[TOPK - 1:TOPK, :]
            cmax = tv[0:1, :]
            z = jnp.sum(jnp.where(cand >= tau, jnp.exp(cand - cmax), 0.0), axis=0, keepdims=True)
            top_hi = jnp.sum(jnp.where(v1[0:1, :] + v2[hs:, :] >= tau, 1.0, 0.0), axis=0, keepdims=True)
            nrow = jnp.where(a1 == v1[0:1, :], top_hi, 0.0)
            for b in range(hs):
                nrow = nrow + jnp.where(a1 + v2[b:b + 1, :] >= tau, 1.0, 0.0)
            rows = (_pair_bf16(nrow), _pair_bf16(jnp.exp(a1 - v1[0:1, :]) / z))
            tiles = (pltpu.bitcast(rho2.astype(BF16), jnp.uint32),
                     pltpu.bitcast(jnp.exp(a2 - v2[0:1, :]).astype(BF16), jnp.uint32))
            for which in range(2):
                for g in range(rows[which].shape[0] // SUBLANES):
                    np_ref[c, g, hd, which] = rows[which][g * SUBLANES:(g + 1) * SUBLANES, :]
                for m in range(tiles[which].shape[0] // SUBLANES):
                    rp_ref[c, m, hd, which] = tiles[which][m * SUBLANES:(m + 1) * SUBLANES, :]
        return carry

    lax.fori_loop(0, heads, head_body, 0)


def _route(h2d, g_ffn, wqt, k1, k2, *, tm):
    t, d = h2d.shape
    heads, nk, dk = k1.shape[0], 2 * k1.shape[1], k1.shape[2]
    assert t % tm == 0 and tm % LANES == 0
    np_dims = (nk // SUBLANES, heads, 2, SUBLANES, LANES)
    rp_dims = (nk // (2 * SUBLANES), heads, 2, SUBLANES, LANES)
    side_spec = lambda dims: pl.BlockSpec((tm // LANES,) + dims, lambda i: (i, 0, 0, 0, 0, 0))
    side_shape = lambda dims: jax.ShapeDtypeStruct((t // LANES,) + dims, jnp.uint32)
    return pl.pallas_call(
        functools.partial(_route_kernel, heads=heads, dk=dk, tm=tm),
        grid=(t // tm,),
        in_specs=[pl.BlockSpec((tm, d), lambda i: (i, 0)),
                  pl.BlockSpec((1, d), lambda i: (0, 0)),
                  pl.BlockSpec(wqt.shape, lambda i: (0, 0)),
                  pl.BlockSpec(k1.shape, lambda i: (0, 0, 0)),
                  pl.BlockSpec(k2.shape, lambda i: (0, 0, 0))],
        out_specs=[pl.BlockSpec((d // 2, tm), lambda i: (0, i)), side_spec(np_dims), side_spec(rp_dims)],
        out_shape=[jax.ShapeDtypeStruct((d // 2, t), jnp.uint32), side_shape(np_dims), side_shape(rp_dims)],
        scratch_shapes=[pltpu.VMEM((2 * wqt.shape[0], tm), F32)],
        compiler_params=pltpu.CompilerParams(
            dimension_semantics=("arbitrary",), vmem_limit_bytes=VMEM_LIMIT),
        name="route",
    )(h2d, g_ffn.reshape(1, d), wqt, k1, k2)


def _peer_kernel(xn_ref, h_ref, u_ref, vt_ref, np_ref, rp_ref, p_ref, gple_ref, wpg_ref, wpe_ref,
                 gfin_ref, o_ref, acc, s_a, s_b, act_a, act_b, *, heads, nk, tm, te, n_tiles, final):
    j = pl.program_id(1)
    half = te // 2
    rows_half = half // nk
    pack = 2 * SUBLANES

    def scores(hf, s_buf):
        u_half = pltpu.bitcast(u_ref[hf * half // 2:(hf + 1) * half // 2, :], BF16)
        s = _dot(u_half, pltpu.bitcast(xn_ref[...], BF16))
        for c in range(tm // LANES):
            s_buf[c] = s[:, c * LANES:(c + 1) * LANES]

    def accumulate(act_buf, hf):
        vt_half = pltpu.bitcast(vt_ref[:, hf * half:(hf + 1) * half], BF16)
        act = jnp.concatenate([pltpu.bitcast(act_buf[c], BF16) for c in range(tm // LANES)], axis=1)
        acc[...] += _dot(vt_half, act)

    def gate(s_buf, act_buf, tile, hf):
        for c in range(tm // LANES):
            nr8 = [np_ref[c, tile, hd, 0] for hd in range(heads)]
            pr8 = [np_ref[c, tile, hd, 1] for hd in range(heads)]
            for k in range(rows_half):
                kk = hf * rows_half + k
                nr = [pltpu.bitcast(jnp.broadcast_to(nr8[hd][kk:kk + 1, :], (SUBLANES, LANES)), BF16)
                      for hd in range(heads)]
                pr = [pltpu.bitcast(jnp.broadcast_to(pr8[hd][kk:kk + 1, :], (SUBLANES, LANES)), BF16)
                      for hd in range(heads)]
                for m in range(nk // pack):
                    w = None
                    for hd in range(heads):
                        keep = pltpu.bitcast(rp_ref[c, m, hd, 0], BF16) < nr[hd]
                        p2 = pltpu.bitcast(rp_ref[c, m, hd, 1], BF16)
                        term = jnp.where(keep, p2, jnp.zeros((), BF16)) * pr[hd]
                        w = term if w is None else w + term
                    rs = slice(k * nk + m * pack, k * nk + (m + 1) * pack)
                    act = _gelu_tanh(s_buf[c, rs, :].astype(BF16)) * w
                    ws = slice((k * nk + m * pack) // 2, (k * nk + (m + 1) * pack) // 2)
                    act_buf[c, ws, :] = pltpu.bitcast(act, jnp.uint32)

    first = j == 0
    steady, steady_b = jnp.logical_and(j > 0, j < n_tiles), jnp.logical_and(j >= 1, j <= n_tiles - 1)
    last = j == n_tiles

    @pl.when(first)
    def _():
        acc[...] = jnp.zeros(acc.shape, F32)
        scores(0, s_a)

    @pl.when(first)
    def _():
        scores(1, s_b)
        gate(s_a, act_a, j, 0)

    @pl.when(steady)
    def _():
        gate(s_b, act_b, j - 1, 1)
        scores(0, s_a)
        accumulate(act_a, 0)

    @pl.when(steady_b)
    def _():
        gate(s_a, act_a, j, 0)
        scores(1, s_b)
        accumulate(act_b, 1)

    @pl.when(last)
    def _():
        gate(s_b, act_b, j - 1, 1)
        accumulate(act_a, 0)

    @pl.when(last)
    def _():
        accumulate(act_b, 1)
        out = h_ref[...] + acc[...].T
        ple_gate = jax.nn.sigmoid(_dot(_rmsnorm(out, gple_ref[...]).astype(BF16),
                                       pltpu.bitcast(wpg_ref[...], BF16)))
        out = out + _dot(p_ref[...].astype(BF16), pltpu.bitcast(wpe_ref[...], BF16)) * ple_gate
        if final:
            out = _rmsnorm(out, gfin_ref[...])
        o_ref[...] = out


def _peer(xn, h2d, u_bf, vt_bf, np_side, rp_side, p3d, layer, g_ple, wpg, wpe, g_final, *, tm, te, final):
    t, d = h2d.shape
    n_exp = 2 * u_bf.shape[0]
    nk, heads = np_side.shape[1] * SUBLANES, np_side.shape[2]
    assert te == nk * SUBLANES and n_exp == nk * nk and t % tm == 0 and tm % LANES == 0
    n_tiles = n_exp // te
    half = te // 2
    side_spec = lambda a: pl.BlockSpec((tm // LANES,) + a.shape[1:], lambda i, j: (i, 0, 0, 0, 0, 0))
    return pl.pallas_call(
        functools.partial(_peer_kernel, heads=heads, nk=nk, tm=tm, te=te, n_tiles=n_tiles, final=final),
        grid=(t // tm, n_tiles + 1),
        in_specs=[pl.BlockSpec((d // 2, tm), lambda i, j: (0, i)),
                  pl.BlockSpec((tm, d), lambda i, j: (i, 0)),
                  pl.BlockSpec((te // 2, d), lambda i, j: (jnp.minimum(j, n_tiles - 1), 0)),
                  pl.BlockSpec((d // 2, te), lambda i, j: (0, jnp.maximum(j - 1, 0))),
                  side_spec(np_side), side_spec(rp_side),
                  pl.BlockSpec((None, tm, p3d.shape[2]), lambda i, j: (layer, i, 0)),
                  pl.BlockSpec((1, d), lambda i, j: (0, 0)),
                  pl.BlockSpec(wpg.shape, lambda i, j: (0, 0)),
                  pl.BlockSpec(wpe.shape, lambda i, j: (0, 0)),
                  pl.BlockSpec((1, d), lambda i, j: (0, 0))],
        out_specs=pl.BlockSpec((tm, d), lambda i, j: (i, 0)),
        out_shape=jax.ShapeDtypeStruct((t, d), F32),
        scratch_shapes=[pltpu.VMEM((d, tm), F32),
                        pltpu.VMEM((tm // LANES, half, LANES), F32),
                        pltpu.VMEM((tm // LANES, half, LANES), F32),
                        pltpu.VMEM((tm // LANES, half // 2, LANES), jnp.uint32),
                        pltpu.VMEM((tm // LANES, half // 2, LANES), jnp.uint32)],
        compiler_params=pltpu.CompilerParams(
            dimension_semantics=("arbitrary", "arbitrary"), vmem_limit_bytes=VMEM_LIMIT),
        name="peer_final" if final else "peer",
    )(xn, h2d, u_bf, vt_bf, np_side, rp_side, p3d, g_ple.reshape(1, d), wpg, wpe, g_final.reshape(1, d))


def kernel(x, p, g_mix, w_in, conv_a_w, conv_a_b, ln_a_g, ln_a_b, conv_b_w, ln_c_g, ln_c_b, w_s, b_s, w_out, g_ffn, w_q, sub_keys, expert_u, expert_v, g_ple, w_pe, w_pg, g_final):
    depth = w_in.shape[0]
    bsz, seq, d = x.shape
    t = bsz * seq
    d_c = ln_c_g.shape[-1]
    heads_c = d_c // HEAD_DIM

    lane = jnp.arange(d_c)
    gavg = jnp.where(lane[:, None] // HEAD_DIM == lane[None, :] // HEAD_DIM, 1.0 / HEAD_DIM, 0.0).astype(BF16)
    pos = jnp.arange(SG_BLOCK)
    causal = (pos[None, :] // CHUNK) <= (pos[:, None] // CHUNK)

    h = x
    for i in range(depth):
        wss = jnp.where(causal[None], w_s[i], 0.0).astype(BF16).reshape(heads_c * SG_BLOCK, SG_BLOCK)
        bsx = jnp.repeat(b_s[i].T, HEAD_DIM, axis=1)
        h = _mixer(h, g_mix[i], w_in[i].astype(BF16), conv_a_w[i], conv_a_b[i], ln_a_g[i], ln_a_b[i],
                   conv_b_w[i], ln_c_g[i], ln_c_b[i], wss, bsx, gavg, w_out[i].astype(BF16), ts=512)
        h2d = h.reshape(t, d)
        n_heads, _, n_keys, dk = sub_keys[i].shape
        keys = _pack_rows(jnp.swapaxes(sub_keys[i], 0, 1).reshape(2 * n_heads * n_keys, dk))
        keys = keys.reshape(2, n_heads, n_keys // 2, dk)
        xn, np_side, rp_side = _route(h2d, g_ffn[i], _pack_rows(w_q, layer=i, transpose=True),
                                       keys[0], keys[1], tm=512)
        h2d = _peer(xn, h2d, _pack_rows(expert_u, layer=i), _pack_rows(expert_v, layer=i, transpose=True),
                    np_side, rp_side, p.reshape(depth, t, -1), i, g_ple[i], _pack_rows(w_pg, layer=i),
                    _pack_rows(w_pe, layer=i), g_final, tm=512, te=1024, final=(i == depth - 1))
        h = h2d.reshape(bsz, seq, d)
    return h
```

```python
import functools

import jax
import jax.numpy as jnp
from jax import lax
from jax.experimental import pallas as pl
from jax.experimental.pallas import tpu as pltpu

EPS = 1e-6
HEAD_DIM = 64
CHUNK = 64
SG_BLOCK = 128
TOPK = 16
LANES = 128
SUBLANES = 8
HALO_A = 32
HALO_B = 8
VMEM_LIMIT = 56 * 1024 * 1024

MIXER_SEQ_TILE = 512
CONV_ROWS = 64
ROUTE_TOKEN_TILE = 512
PEER_TOKEN_TILE = 512

F32 = jnp.float32
BF16 = jnp.bfloat16


def _dot(a, b):
    return jnp.dot(a, b, preferred_element_type=F32)


def _pack_kernel(w_ref, o_ref, *, transpose):
    w = w_ref[...]
    if transpose:
        w = w.T
    o_ref[...] = pltpu.bitcast(w.astype(BF16), jnp.uint32)


def _pack_rows(w, *, layer=None, transpose=False, rows=512):
    n, d = w.shape[-2:]
    rows = min(rows, n)
    assert n % rows == 0 and rows % (2 * SUBLANES) == 0 and (w.ndim == 3) == (layer is not None)
    if layer is None:
        in_spec = pl.BlockSpec((rows, d), lambda i: (i, 0))
    else:
        in_spec = pl.BlockSpec((None, rows, d), lambda i: (layer, i, 0))
    if transpose:
        out_spec = pl.BlockSpec((d // 2, rows), lambda i: (0, i))
        out_shape = jax.ShapeDtypeStruct((d // 2, n), jnp.uint32)
    else:
        out_spec = pl.BlockSpec((rows // 2, d), lambda i: (i, 0))
        out_shape = jax.ShapeDtypeStruct((n // 2, d), jnp.uint32)
    return pl.pallas_call(
        functools.partial(_pack_kernel, transpose=transpose),
        grid=(n // rows,), in_specs=[in_spec], out_specs=out_spec, out_shape=out_shape,
        compiler_params=pltpu.CompilerParams(
            dimension_semantics=("arbitrary",), vmem_limit_bytes=VMEM_LIMIT),
        name="pack_t" if transpose else "pack",
    )(w)


def _rmsnorm(x, g):
    return x * lax.rsqrt(jnp.mean(x * x, axis=-1, keepdims=True) + EPS) * g


def _dot_hilo(x, w):
    hi = x.astype(BF16)
    lo = (x - hi.astype(F32)).astype(BF16)
    return _dot(hi, w) + _dot(lo, w)


def _group_layernorm(x, gavg, g, b):
    mean = _dot_hilo(x, gavg)
    d = x - mean
    var = _dot_hilo(d * d, gavg)
    return d * lax.rsqrt(var + EPS) * g + b


def _gelu_tanh(x):
    c = 0.7978845608028654
    h = 0.5 * x
    return h + h * jnp.tanh(x * (c + (c * 0.044715) * (x * x)))


def _mixer_kernel(h_ref, gmix_ref, win_ref, caw_ref, cab_ref, lag_ref, lab_ref, cbw_ref,
                  lcg_ref, lcb_ref, wss_ref, bsx_ref, gavg_ref, wout_ref, o_ref,
                  ybuf, cbuf, ya_buf, ysh, *, d_a, d_b, d_c, ka, kb, ts, rows):
    @pl.when(pl.program_id(1) == 0)
    def _():
        ybuf[0:HALO_A, :] = jnp.zeros((HALO_A, d_a), F32)
        cbuf[0:HALO_B, :] = jnp.zeros((HALO_B, d_b), F32)

    h = h_ref[0]
    a = _rmsnorm(h, gmix_ref[...]).astype(BF16)
    z = _dot(a, win_ref[...])
    o_b = 2 * d_a
    o_c = o_b + 3 * d_b

    ybuf[HALO_A:HALO_A + ts, :] = z[:, :d_a] * jax.nn.sigmoid(z[:, d_a:2 * d_a])
    n_sh = ysh.shape[1]
    for s in range(1, SUBLANES):
        ysh[s - 1] = ybuf[s:s + n_sh, :]
    for r0 in range(0, ts, rows):
        acc = jnp.zeros((rows, d_a), F32) + cab_ref[...]
        for k in range(ka):
            off = HALO_A - (ka - 1) + k + r0
            s = off % SUBLANES
            win = ybuf[off:off + rows, :] if s == 0 else ysh[s - 1, off - s:off - s + rows, :]
            acc = acc + win * caw_ref[k:k + 1, :]
        ya_buf[r0:r0 + rows, :] = acc
    ybuf[0:HALO_A, :] = ybuf[ts:ts + HALO_A, :]
    ya = _group_layernorm(ya_buf[...], gavg_ref[...], lag_ref[...], lab_ref[...])
    ya = ya * jax.nn.sigmoid(ya)

    cbuf[HALO_B:HALO_B + ts, :] = z[:, o_b + d_b:o_b + 2 * d_b] * z[:, o_b + 2 * d_b:o_c]
    cb = jnp.zeros((ts, d_b), F32)
    for k in range(kb):
        off = HALO_B - (kb - 1) + k
        cb = cb + cbuf[off:off + ts, :] * cbw_ref[k:k + 1, :]
    cbuf[0:HALO_B, :] = cbuf[ts:ts + HALO_B, :]
    yb = z[:, o_b:o_b + d_b] * cb

    u = z[:, o_c:o_c + d_c]
    vln = _group_layernorm(z[:, o_c + d_c:], gavg_ref[...], lcg_ref[...], lcb_ref[...]).astype(BF16)
    heads_c = d_c // HEAD_DIM
    lane_head = lax.broadcasted_iota(jnp.int32, (SG_BLOCK, d_c), 1) // HEAD_DIM
    yc_blocks = []
    for n in range(ts // SG_BLOCK):
        sl = slice(n * SG_BLOCK, (n + 1) * SG_BLOCK)
        r = _dot(wss_ref[...], vln[sl, :])
        mixed = bsx_ref[...]
        for hd in range(heads_c):
            mixed = mixed + jnp.where(lane_head == hd, r[hd * SG_BLOCK:(hd + 1) * SG_BLOCK, :], 0.0)
        yc_blocks.append(u[sl, :] * mixed)
    yc = jnp.concatenate(yc_blocks, axis=0)

    ycat = jnp.concatenate([ya, yb, yc], axis=1).astype(BF16)
    o_ref[0] = h + _dot(ycat, wout_ref[...])


def _mixer(h, g_mix, w_in, conv_a_w, conv_a_b, ln_a_g, ln_a_b, conv_b_w, ln_c_g, ln_c_b,
           wss, bsx, gavg, w_out, *, ts):
    bsz, seq, d = h.shape
    ka, d_a = conv_a_w.shape
    kb, d_b = conv_b_w.shape
    d_c = ln_c_g.shape[-1]
    assert d_a == d_c and seq % ts == 0 and ts % SG_BLOCK == 0
    row = lambda v: v.reshape(1, -1)
    full = lambda arr: pl.BlockSpec(arr.shape, lambda b, s: (0,) * arr.ndim)
    operands = (row(g_mix), w_in, conv_a_w, row(conv_a_b), row(ln_a_g), row(ln_a_b), conv_b_w,
                row(ln_c_g), row(ln_c_b), wss, bsx, gavg, w_out)
    blk = pl.BlockSpec((1, ts, d), lambda b, s: (b, s, 0))
    return pl.pallas_call(
        functools.partial(_mixer_kernel, d_a=d_a, d_b=d_b, d_c=d_c, ka=ka, kb=kb, ts=ts, rows=CONV_ROWS),
        grid=(bsz, seq // ts),
        in_specs=[blk] + [full(o) for o in operands],
        out_specs=blk,
        out_shape=jax.ShapeDtypeStruct(h.shape, F32),
        scratch_shapes=[pltpu.VMEM((ts + HALO_A, d_a), F32),
                        pltpu.VMEM((ts + HALO_B, d_b), F32),
                        pltpu.VMEM((ts, d_a), F32),
                        pltpu.VMEM((SUBLANES - 1, ts + HALO_A - SUBLANES, d_a), F32)],
        compiler_params=pltpu.CompilerParams(
            dimension_semantics=("arbitrary", "arbitrary"), vmem_limit_bytes=VMEM_LIMIT),
        name="mixer",
    )(h, *operands)


def _merge_exchange_pairs(n):
    pairs = []
    t = (n - 1).bit_length()
    p = 1 << (t - 1)
    while p > 0:
        q, r, d = 1 << (t - 1), 0, p
        while True:
            pairs.extend((i, i + d) for i in range(n - d) if (i & p) == r)
            if q == p:
                break
            d, q, r = q - p, q >> 1, p
        p >>= 1
    return pairs


def _top_values(s, k):
    n = s.shape[0] // SUBLANES
    lists = [s[i * SUBLANES:(i + 1) * SUBLANES, :] for i in range(n)]
    for i, j in _merge_exchange_pairs(n):
        lists[i], lists[j] = jnp.maximum(lists[i], lists[j]), jnp.minimum(lists[i], lists[j])
    vals = []
    for r in range(k):
        m = jnp.max(lists[0], axis=0, keepdims=True)
        vals.append(m)
        if r + 1 < k:
            head = lists[0] == m
            for i in range(min(n - 1, k - 1 - r)):
                lists[i] = jnp.where(head, lists[i + 1], lists[i])
            if n - 1 < k - 1 - r:
                lists[n - 1] = jnp.where(head, -jnp.inf, lists[n - 1])
    return jnp.concatenate(vals, axis=0)


def _pair_bf16(x):
    b = pltpu.bitcast(x.astype(BF16).astype(F32), jnp.uint32)
    return b | (b >> 16)


def _route_kernel(h_ref, g_ref, wqt_ref, k1_ref, k2_ref,
                  xn_ref, np_ref, rp_ref, q_scr, *, heads, dk, tm):
    xnt = _rmsnorm(h_ref[...], g_ref[...]).T.astype(BF16)
    xn_ref[...] = pltpu.bitcast(xnt, jnp.uint32)
    q_scr[...] = _dot(pltpu.bitcast(wqt_ref[...], BF16), xnt)

    def head_body(hd, carry):
        base = pl.multiple_of(hd * (2 * dk), 2 * dk)
        q1 = q_scr[pl.ds(base, dk), :].astype(BF16)
        q2 = q_scr[pl.ds(base + dk, dk), :].astype(BF16)
        s1 = _dot(pltpu.bitcast(k1_ref[hd], BF16), q1)
        s2 = _dot(pltpu.bitcast(k2_ref[hd], BF16), q2)
        for c in range(tm // LANES):
            cs = slice(c * LANES, (c + 1) * LANES)
            a1 = s1[:, cs]
            a2 = s2[:, cs]
            v1 = _top_values(a1, TOPK)
            v2 = _top_values(a2, TOPK)
            rho2 = jnp.full(a2.shape, float(TOPK), F32)
            for r in reversed(range(TOPK)):
                rho2 = jnp.where(a2 >= v2[r:r + 1, :], float(r), rho2)
            hs = TOPK // 2
            cand = jnp.concatenate(
                [v1[0:1, :] + v2]
                + [v1[a:a + 1, :] + v2[0:hs, :] for a in range(1, hs)]
                + [v1[hs:, :] + v2[0:1, :]], axis=0)
            tv = _top_values(cand, TOPK)
            tau = tv[TOPK - 1:TOPK, :]
            cmax = tv[0:1, :]
            z = jnp.sum(jnp.where(cand >= tau, jnp.exp(cand - cmax), 0.0), axis=0, keepdims=True)
            top_hi = jnp.sum(jnp.where(v1[0:1, :] + v2[hs:, :] >= tau, 1.0, 0.0), axis=0, keepdims=True)
            nrow = jnp.where(a1 == v1[0:1, :], top_hi, 0.0)
            for b in range(hs):
                nrow = nrow + jnp.where(a1 + v2[b:b + 1, :] >= tau, 1.0, 0.0)
            rows = (_pair_bf16(nrow), _pair_bf16(jnp.exp(a1 - v1[0:1, :]) / z))
            tiles = (pltpu.bitcast(rho2.astype(BF16), jnp.uint32),
                     pltpu.bitcast(jnp.exp(a2 - v2[0:1, :]).astype(BF16), jnp.uint32))
            for which in range(2):
                for g in range(rows[which].shape[0] // SUBLANES):
                    np_ref[c, g, hd, which] = rows[which][g * SUBLANES:(g + 1) * SUBLANES, :]
                for m in range(tiles[which].shape[0] // SUBLANES):
                    rp_ref[c, m, hd, which] = tiles[which][m * SUBLANES:(m + 1) * SUBLANES, :]
        return carry

    lax.fori_loop(0, heads, head_body, 0)


def _route(h2d, g_ffn, wqt, k1, k2, *, tm):
    t, d = h2d.shape
    heads, nk, dk = k1.shape[0], 2 * k1.shape[1], k1.shape[2]
    assert t % tm == 0 and tm % LANES == 0
    np_dims = (nk // SUBLANES, heads, 2, SUBLANES, LANES)
    rp_dims = (nk // (2 * SUBLANES), heads, 2, SUBLANES, LANES)
    side_spec = lambda dims: pl.BlockSpec((tm // LANES,) + dims, lambda i: (i, 0, 0, 0, 0, 0))
    side_shape = lambda dims: jax.ShapeDtypeStruct((t // LANES,) + dims, jnp.uint32)
    return pl.pallas_call(
        functools.partial(_route_kernel, heads=heads, dk=dk, tm=tm),
        grid=(t // tm,),
        in_specs=[pl.BlockSpec((tm, d), lambda i: (i, 0)),
                  pl.BlockSpec((1, d), lambda i: (0, 0)),
                  pl.BlockSpec(wqt.shape, lambda i: (0, 0)),
                  pl.BlockSpec(k1.shape, lambda i: (0, 0, 0)),
                  pl.BlockSpec(k2.shape, lambda i: (0, 0, 0))],
        out_specs=[pl.BlockSpec((d // 2, tm), lambda i: (0, i)), side_spec(np_dims), side_spec(rp_dims)],
        out_shape=[jax.ShapeDtypeStruct((d // 2, t), jnp.uint32), side_shape(np_dims), side_shape(rp_dims)],
        scratch_shapes=[pltpu.VMEM((2 * wqt.shape[0], tm), F32)],
        compiler_params=pltpu.CompilerParams(
            dimension_semantics=("arbitrary",), vmem_limit_bytes=VMEM_LIMIT),
        name="route",
    )(h2d, g_ffn.reshape(1, d), wqt, k1, k2)


def _peer_kernel(xn_ref, h_ref, u_ref, vt_ref, np_ref, rp_ref, p_ref, gple_ref, wpg_ref, wpe_ref,
                 gfin_ref, o_ref, acc, s_a, s_b, act_a, act_b, *, heads, nk, tm, te, n_tiles, final):
    j = pl.program_id(1)
    half = te // 2
    rows_half = half // nk
    pack = 2 * SUBLANES

    def scores(hf, s_buf):
        u_half = pltpu.bitcast(u_ref[hf * half // 2:(hf + 1) * half // 2, :], BF16)
        s = _dot(u_half, pltpu.bitcast(xn_ref[...], BF16))
        for c in range(tm // LANES):
            s_buf[c] = s[:, c * LANES:(c + 1) * LANES]

    def accumulate(act_buf, hf):
        vt_half = pltpu.bitcast(vt_ref[:, hf * half:(hf + 1) * half], BF16)
        act = jnp.concatenate([pltpu.bitcast(act_buf[c], BF16) for c in range(tm // LANES)], axis=1)
        acc[...] += _dot(vt_half, act)

    def gate(s_buf, act_buf, tile, hf):
        for c in range(tm // LANES):
            nr8 = [np_ref[c, tile, hd, 0] for hd in range(heads)]
            pr8 = [np_ref[c, tile, hd, 1] for hd in range(heads)]
            for k in range(rows_half):
                kk = hf * rows_half + k
                nr = [pltpu.bitcast(jnp.broadcast_to(nr8[hd][kk:kk + 1, :], (SUBLANES, LANES)), BF16)
                      for hd in range(heads)]
                pr = [pltpu.bitcast(jnp.broadcast_to(pr8[hd][kk:kk + 1, :], (SUBLANES, LANES)), BF16)
                      for hd in range(heads)]
                for m in range(nk // pack):
                    w = None
                    for hd in range(heads):
                        keep = pltpu.bitcast(rp_ref[c, m, hd, 0], BF16) < nr[hd]
                        p2 = pltpu.bitcast(rp_ref[c, m, hd, 1], BF16)
                        term = jnp.where(keep, p2, jnp.zeros((), BF16)) * pr[hd]
                        w = term if w is None else w + term
                    rs = slice(k * nk + m * pack, k * nk + (m + 1) * pack)
                    act = _gelu_tanh(s_buf[c, rs, :].astype(BF16)) * w
                    ws = slice((k * nk + m * pack) // 2, (k * nk + (m + 1) * pack) // 2)
                    act_buf[c, ws, :] = pltpu.bitcast(act, jnp.uint32)

    first = j == 0
    steady, steady_b = jnp.logical_and(j > 0, j < n_tiles), jnp.logical_and(j >= 1, j <= n_tiles - 1)
    last = j == n_tiles

    @pl.when(first)
    def _():
        acc[...] = jnp.zeros(acc.shape, F32)
        scores(0, s_a)

    @pl.when(first)
    def _():
        scores(1, s_b)
        gate(s_a, act_a, j, 0)

    @pl.when(steady)
    def _():
        gate(s_b, act_b, j - 1, 1)
        scores(0, s_a)
        accumulate(act_a, 0)

    @pl.when(steady_b)
    def _():
        gate(s_a, act_a, j, 0)
        scores(1, s_b)
        accumulate(act_b, 1)

    @pl.when(last)
    def _():
        gate(s_b, act_b, j - 1, 1)
        accumulate(act_a, 0)

    @pl.when(last)
    def _():
        accumulate(act_b, 1)
        out = h_ref[...] + acc[...].T
        ple_gate = jax.nn.sigmoid(_dot(_rmsnorm(out, gple_ref[...]).astype(BF16),
                                       pltpu.bitcast(wpg_ref[...], BF16)))
        out = out + _dot(p_ref[...].astype(BF16), pltpu.bitcast(wpe_ref[...], BF16)) * ple_gate
        if final:
            out = _rmsnorm(out, gfin_ref[...])
        o_ref[...] = out


def _peer(xn, h2d, u_bf, vt_bf, np_side, rp_side, p3d, layer, g_ple, wpg, wpe, g_final, *, tm, final):
    t, d = h2d.shape
    n_exp = 2 * u_bf.shape[0]
    nk, heads = np_side.shape[1] * SUBLANES, np_side.shape[2]
    te = nk * SUBLANES
    assert n_exp == nk * nk and t % tm == 0 and tm % LANES == 0
    n_tiles = n_exp // te
    half = te // 2
    side_spec = lambda a: pl.BlockSpec((tm // LANES,) + a.shape[1:], lambda i, j: (i, 0, 0, 0, 0, 0))
    return pl.pallas_call(
        functools.partial(_peer_kernel, heads=heads, nk=nk, tm=tm, te=te, n_tiles=n_tiles, final=final),
        grid=(t // tm, n_tiles + 1),
        in_specs=[pl.BlockSpec((d // 2, tm), lambda i, j: (0, i)),
                  pl.BlockSpec((tm, d), lambda i, j: (i, 0)),
                  pl.BlockSpec((te // 2, d), lambda i, j: (jnp.minimum(j, n_tiles - 1), 0)),
                  pl.BlockSpec((d // 2, te), lambda i, j: (0, jnp.maximum(j - 1, 0))),
                  side_spec(np_side), side_spec(rp_side),
                  pl.BlockSpec((None, tm, p3d.shape[2]), lambda i, j: (layer, i, 0)),
                  pl.BlockSpec((1, d), lambda i, j: (0, 0)),
                  pl.BlockSpec(wpg.shape, lambda i, j: (0, 0)),
                  pl.BlockSpec(wpe.shape, lambda i, j: (0, 0)),
                  pl.BlockSpec((1, d), lambda i, j: (0, 0))],
        out_specs=pl.BlockSpec((tm, d), lambda i, j: (i, 0)),
        out_shape=jax.ShapeDtypeStruct((t, d), F32),
        scratch_shapes=[pltpu.VMEM((d, tm), F32),
                        pltpu.VMEM((tm // LANES, half, LANES), F32),
                        pltpu.VMEM((tm // LANES, half, LANES), F32),
                        pltpu.VMEM((tm // LANES, half // 2, LANES), jnp.uint32),
                        pltpu.VMEM((tm // LANES, half // 2, LANES), jnp.uint32)],
        compiler_params=pltpu.CompilerParams(
            dimension_semantics=("arbitrary", "arbitrary"), vmem_limit_bytes=VMEM_LIMIT),
        name="peer_final" if final else "peer",
    )(xn, h2d, u_bf, vt_bf, np_side, rp_side, p3d, g_ple.reshape(1, d), wpg, wpe, g_final.reshape(1, d))


def kernel(x, p, g_mix, w_in, conv_a_w, conv_a_b, ln_a_g, ln_a_b, conv_b_w, ln_c_g, ln_c_b, w_s, b_s, w_out, g_ffn, w_q, sub_keys, expert_u, expert_v, g_ple, w_pe, w_pg, g_final):
    depth = w_in.shape[0]
    bsz, seq, d = x.shape
    t = bsz * seq
    d_c = ln_c_g.shape[-1]
    heads_c = d_c // HEAD_DIM

    lane = jnp.arange(d_c)
    gavg = jnp.where(lane[:, None] // HEAD_DIM == lane[None, :] // HEAD_DIM, 1.0 / HEAD_DIM, 0.0).astype(BF16)
    pos = jnp.arange(SG_BLOCK)
    causal = (pos[None, :] // CHUNK) <= (pos[:, None] // CHUNK)

    h = x
    for i in range(depth):
        wss = jnp.where(causal[None], w_s[i], 0.0).astype(BF16).reshape(heads_c * SG_BLOCK, SG_BLOCK)
        bsx = jnp.repeat(b_s[i].T, HEAD_DIM, axis=1)
        h = _mixer(h, g_mix[i], w_in[i].astype(BF16), conv_a_w[i], conv_a_b[i], ln_a_g[i], ln_a_b[i],
                   conv_b_w[i], ln_c_g[i], ln_c_b[i], wss, bsx, gavg, w_out[i].astype(BF16), ts=MIXER_SEQ_TILE)
        h2d = h.reshape(t, d)
        n_heads, _, n_keys, dk = sub_keys[i].shape
        keys = _pack_rows(jnp.swapaxes(sub_keys[i], 0, 1).reshape(2 * n_heads * n_keys, dk))
        keys = keys.reshape(2, n_heads, n_keys // 2, dk)
        xn, np_side, rp_side = _route(h2d, g_ffn[i], _pack_rows(w_q, layer=i, transpose=True),
                                       keys[0], keys[1], tm=ROUTE_TOKEN_TILE)
        h2d = _peer(xn, h2d, _pack_rows(expert_u, layer=i), _pack_rows(expert_v, layer=i, transpose=True),
                    np_side, rp_side, p.reshape(depth, t, -1), i, g_ple[i], _pack_rows(w_pg, layer=i),
                    _pack_rows(w_pe, layer=i), g_final, tm=PEER_TOKEN_TILE, final=(i == depth - 1))
        h = h2d.reshape(bsz, seq, d)
    return h
```
